```python
import jax, jax.numpy as jnp
from jax import lax
import numpy as np

D_MODEL = 1024
BATCH = 2
SEQ = 8192
DEPTH = 1

PLE_DIM = 256
HEAD_DIM = 64
N_ATTN_HEADS = 8
D_ATTN = N_ATTN_HEADS * HEAD_DIM
D_CONV = D_MODEL // 2
D_MIX = D_ATTN + D_CONV
D_IN = 3 * D_ATTN + 2 * D_CONV
CONV_WIDTH = 31
ROPE_DIM = HEAD_DIM // 4
ROPE_THETA = 500000.0
DILATED_PATTERNS = ((128, 1), (512, 4), (2048, 16))
ATTN_BLOCK = 128
N_KEYS = 128
N_EXPERTS = N_KEYS * N_KEYS
PEER_HEADS = 8
PEER_TOPK = 16
D_KEY = 256
PEER_BLOCK = 128
EPS = 1e-6

kernel_name = "hymba_dilated_conformer_peer_layer"


def _rmsnorm(x, g):
    xf = x.astype(jnp.float32)
    y = xf * lax.rsqrt(jnp.mean(xf * xf, axis=-1, keepdims=True) + EPS)
    return (y * g.astype(jnp.float32)).astype(x.dtype)


def _layernorm(x, g, b):
    xf = x.astype(jnp.float32)
    mu = jnp.mean(xf, axis=-1, keepdims=True)
    var = jnp.mean(jnp.square(xf - mu), axis=-1, keepdims=True)
    y = (xf - mu) * lax.rsqrt(var + EPS)
    return (y * g.astype(jnp.float32) + b.astype(jnp.float32)).astype(x.dtype)


def _partial_rope(t, positions):
    inv_freq = ROPE_THETA ** (-jnp.arange(0, ROPE_DIM, 2, dtype=jnp.float32) / ROPE_DIM)
    ang = positions.astype(jnp.float32)[..., None] * inv_freq
    cos = jnp.cos(ang)[:, :, None, :]
    sin = jnp.sin(ang)[:, :, None, :]
    tr = t[..., :ROPE_DIM].astype(jnp.float32)
    t1, t2 = tr[..., :ROPE_DIM // 2], tr[..., ROPE_DIM // 2:]
    rot = jnp.concatenate([t1 * cos - t2 * sin, t2 * cos + t1 * sin], axis=-1).astype(t.dtype)
    return jnp.concatenate([rot, t[..., ROPE_DIM:]], axis=-1)


def _dilated_attention(q, k, v, window, dilation):
    B, S, H, Dh = q.shape
    span = window // dilation
    L = S // dilation
    nb = -(-L // ATTN_BLOCK)
    Lp = nb * ATTN_BLOCK

    def to_sub(t):
        t = t.reshape(B, L, dilation, H, Dh).transpose(0, 2, 3, 1, 4)
        return jnp.pad(t, ((0, 0), (0, 0), (0, 0), (0, Lp - L), (0, 0)))

    def windows(t):
        t = jnp.pad(t, ((0, 0), (0, 0), (0, 0), (ATTN_BLOCK, 0), (0, 0)))
        t = t.reshape(B, dilation, H, nb + 1, ATTN_BLOCK, Dh)
        return jnp.concatenate([t[:, :, :, :-1], t[:, :, :, 1:]], axis=4)

    qb = to_sub(q).reshape(B, dilation, H, nb, ATTN_BLOCK, Dh)
    kw = windows(to_sub(k))
    vw = windows(to_sub(v))
    qi = jnp.arange(ATTN_BLOCK)[:, None]
    kj = jnp.arange(2 * ATTN_BLOCK)[None, :]
    blk = jnp.arange(nb)[:, None, None]
    dist = ATTN_BLOCK + qi - kj
    key_pos = blk * ATTN_BLOCK + kj - ATTN_BLOCK
    mask = (dist >= 0) & (dist <= span) & (key_pos >= 0)
    s = jnp.einsum('bdhnqe,bdhnke->bdhnqk', qb, kw,
                   preferred_element_type=jnp.float32) * (Dh ** -0.5)
    s = jnp.where(mask, s, -jnp.inf)
    m = jnp.max(s, axis=-1, keepdims=True)
    e = jnp.exp(s - m)
    l = jnp.sum(e, axis=-1, keepdims=True)
    o = jnp.einsum('bdhnqk,bdhnke->bdhnqe', e.astype(vw.dtype), vw,
                   preferred_element_type=jnp.float32) / l
    lse = (m + jnp.log(l))[..., 0]

    def from_sub(t):
        t = t.reshape(B, dilation, H, Lp, *t.shape[5:])[:, :, :, :L]
        t = jnp.moveaxis(t, 3, 1)
        return t.reshape(B, S, H, *t.shape[4:])

    return from_sub(o), from_sub(lse)


def _token_mixers(xn, positions, w_in, conv_w, conv_b, conv_ln_g, conv_ln_b,
                  g_attn_out, g_conv_out, w_out):
    B, S, _ = xn.shape
    z = xn @ w_in
    q, k, v, a, gt = jnp.split(z, [D_ATTN, 2 * D_ATTN, 3 * D_ATTN, 3 * D_ATTN + D_CONV], axis=-1)
    heads = lambda t: t.reshape(B, S, N_ATTN_HEADS, HEAD_DIM)
    q = _partial_rope(heads(q), positions)
    k = _partial_rope(heads(k), positions)
    v = heads(v)
    outs, lses = zip(*[_dilated_attention(q, k, v, w, d) for (w, d) in DILATED_PATTERNS])
    alpha = jax.nn.softmax(jnp.stack(lses, axis=0), axis=0)
    attn = jnp.sum(alpha[..., None] * jnp.stack(outs, axis=0), axis=0)
    attn = attn.reshape(B, S, D_ATTN).astype(xn.dtype)
    glu = a * jax.nn.sigmoid(gt)
    c = lax.conv_general_dilated(glu, conv_w[:, None, :], window_strides=(1,),
                                 padding=[(CONV_WIDTH - 1, 0)],
                                 dimension_numbers=('NWC', 'WIO', 'NWC'),
                                 feature_group_count=D_CONV) + conv_b
    c = jax.nn.silu(_layernorm(c, conv_ln_g, conv_ln_b))
    y = jnp.concatenate([_rmsnorm(attn, g_attn_out), _rmsnorm(c, g_conv_out)], axis=-1)
    return y @ w_out


def _peer(xn, w_q, sub_keys, expert_u, expert_v):
    B, S, D = xn.shape
    N = B * S
    HK = PEER_HEADS * PEER_TOPK
    xt = xn.reshape(N, D)
    q = (xt @ w_q).reshape(N, PEER_HEADS, 2, D_KEY // 2)
    s = jnp.einsum('nhpc,pkc->nhpk', q, sub_keys, preferred_element_type=jnp.float32)
    s_half, i_half = lax.top_k(s, PEER_TOPK)
    cand = (s_half[:, :, 0, :, None] + s_half[:, :, 1, None, :]).reshape(
        N, PEER_HEADS, PEER_TOPK * PEER_TOPK)
    top_s, top_c = lax.top_k(cand, PEER_TOPK)
    i1 = jnp.take_along_axis(i_half[:, :, 0], top_c // PEER_TOPK, axis=-1)
    i2 = jnp.take_along_axis(i_half[:, :, 1], top_c % PEER_TOPK, axis=-1)
    nblk = N // PEER_BLOCK
    expert = (i1 * N_KEYS + i2).reshape(nblk, PEER_BLOCK, HK)
    gate = jax.nn.softmax(top_s, axis=-1).reshape(nblk, PEER_BLOCK, HK)

    def block(args):
        xb, eb, gb = args
        h = jnp.einsum('td,tkd->tk', xb, expert_u[eb], preferred_element_type=jnp.float32)
        act = (jax.nn.gelu(h, approximate=False) * gb).astype(xb.dtype)
        return jnp.einsum('tk,tkd->td', act, expert_v[eb])

    y = lax.map(block, (xt.reshape(nblk, PEER_BLOCK, D), expert, gate))
    return y.reshape(B, S, D)


def setup_inputs(seed: int = 0) -> dict:
    key = jax.random.key(seed)
    ks = jax.random.split(key, 20)
    nrm = lambda k, shape, scale: jax.random.normal(k, shape, jnp.float32) * scale
    gain = lambda k, shape: 1.0 + 0.02 * jax.random.normal(k, shape, jnp.float32)
    return {
        "x": nrm(ks[0], (BATCH, SEQ, D_MODEL), 1.0),
        "p": nrm(ks[1], (DEPTH, BATCH, SEQ, PLE_DIM), 1.0),
        "positions": jnp.broadcast_to(jnp.arange(SEQ, dtype=jnp.int32), (BATCH, SEQ)),
        "norm_mix": gain(ks[2], (DEPTH, D_MODEL)),
        "w_in": nrm(ks[3], (DEPTH, D_MODEL, D_IN), D_MODEL ** -0.5),
        "conv_w": nrm(ks[4], (DEPTH, CONV_WIDTH, D_CONV), CONV_WIDTH ** -0.5),
        "conv_b": nrm(ks[5], (DEPTH, D_CONV), 0.02),
        "conv_ln_g": gain(ks[6], (DEPTH, D_CONV)),
        "conv_ln_b": nrm(ks[7], (DEPTH, D_CONV), 0.02),
        "g_attn_out": gain(ks[8], (DEPTH, D_ATTN)),
        "g_conv_out": gain(ks[9], (DEPTH, D_CONV)),
        "w_out": nrm(ks[10], (DEPTH, D_MIX, D_MODEL), D_MIX ** -0.5),
        "norm_ffn": gain(ks[11], (DEPTH, D_MODEL)),
        "peer_wq": nrm(ks[12], (DEPTH, D_MODEL, PEER_HEADS * D_KEY), D_MODEL ** -0.5),
        "sub_keys": nrm(ks[13], (DEPTH, 2, N_KEYS, D_KEY // 2), (D_KEY // 2) ** -0.5),
        "expert_u": nrm(ks[14], (DEPTH, N_EXPERTS, D_MODEL), D_MODEL ** -0.5),
        "expert_v": nrm(ks[15], (DEPTH, N_EXPERTS, D_MODEL), PEER_HEADS ** -0.5),
        "norm_ple": gain(ks[16], (DEPTH, D_MODEL)),
        "w_ple_gate": nrm(ks[17], (DEPTH, D_MODEL, D_MODEL), D_MODEL ** -0.5),
        "w_ple_proj": nrm(ks[18], (DEPTH, PLE_DIM, D_MODEL), PLE_DIM ** -0.5),
        "final_norm": gain(ks[19], (D_MODEL,)),
    }


def reference(x, p, positions, norm_mix, w_in, conv_w, conv_b, conv_ln_g, conv_ln_b,
              g_attn_out, g_conv_out, w_out, norm_ffn, peer_wq, sub_keys, expert_u,
              expert_v, norm_ple, w_ple_gate, w_ple_proj, final_norm):
    h = x
    for i in range(DEPTH):
        h = h + _token_mixers(_rmsnorm(h, norm_mix[i]), positions, w_in[i], conv_w[i], conv_b[i],
                              conv_ln_g[i], conv_ln_b[i], g_attn_out[i], g_conv_out[i], w_out[i])
        h = h + _peer(_rmsnorm(h, norm_ffn[i]), peer_wq[i], sub_keys[i], expert_u[i], expert_v[i])
        ple_gate = jax.nn.sigmoid(_rmsnorm(h, norm_ple[i]) @ w_ple_gate[i])
        h = h + ple_gate * (p[i] @ w_ple_proj[i])
    return _rmsnorm(h, final_norm)
```

```python
import functools
import math

import jax
import jax.numpy as jnp
from jax import lax
from jax.experimental import pallas as pl
from jax.experimental.pallas import tpu as pltpu

F32 = jnp.float32
BF16 = jnp.bfloat16

D_MODEL = 1024
PLE_DIM = 256
HEAD_DIM = 64
N_ATTN_HEADS = 8
D_ATTN = N_ATTN_HEADS * HEAD_DIM
D_CONV = D_MODEL // 2
D_IN = 3 * D_ATTN + 2 * D_CONV
CONV_WIDTH = 31
ROPE_DIM = HEAD_DIM // 4
ROPE_THETA = 500000.0
DILATED_PATTERNS = ((128, 1), (512, 4), (2048, 16))
ATTN_BLOCK = 128
N_KEYS = 128
N_EXPERTS = N_KEYS * N_KEYS
PEER_HEADS = 8
PEER_TOPK = 16
EPS = 1e-6

LANES = 128
HEADS_PER_LANE_TILE = LANES // HEAD_DIM
CONV_HALO = 32
VMEM_LIMIT = 56 * 1024 * 1024

_STAIR = [(j, k) for j in range(PEER_TOPK) for k in range(PEER_TOPK) if (j + 1) * (k + 1) <= PEER_TOPK]
_STAIR_ROWS = -(-len(_STAIR) // 8) * 8

NEG_INF = float("-inf")


def _params(sem):
    return pltpu.CompilerParams(dimension_semantics=sem, vmem_limit_bytes=VMEM_LIMIT)


def _rms(x, g):
    return x * lax.rsqrt(jnp.mean(x * x, axis=-1, keepdims=True) + EPS) * g


def _mix_in_kernel(x_ref, g_ref, w_ref, cos_ref, sa_ref, sb_ref, q_ref, k_ref, v_ref, glu_ref):
    xn = _rms(x_ref[...], g_ref[...])
    z = jnp.dot(xn.astype(BF16), w_ref[...], preferred_element_type=F32)
    cos, sa, sb = cos_ref[...], sa_ref[...], sb_ref[...]

    def rope(t):
        tiles = []
        for i in range(D_ATTN // LANES):
            tt = t[:, i * LANES:(i + 1) * LANES]
            tiles.append(tt * cos + pltpu.roll(tt, LANES - ROPE_DIM // 2, 1) * sa
                         + pltpu.roll(tt, ROPE_DIM // 2, 1) * sb)
        return jnp.concatenate(tiles, axis=1)

    q_ref[...] = rope(z[:, :D_ATTN]).astype(BF16)
    k_ref[...] = rope(z[:, D_ATTN:2 * D_ATTN]).astype(BF16)
    v_ref[...] = z[:, 2 * D_ATTN:3 * D_ATTN].astype(BF16)
    a = z[:, 3 * D_ATTN:3 * D_ATTN + D_CONV]
    gt = z[:, 3 * D_ATTN + D_CONV:]
    glu_ref[...] = a * jax.nn.sigmoid(gt)


def _mix_in(x, g, w_in, cos_t, sa_t, sb_t, tile):
    B, S, _ = x.shape
    row = lambda d: pl.BlockSpec((None, tile, d), lambda b, i: (b, i, 0))
    full = lambda shape: pl.BlockSpec(shape, lambda b, i: (0,) * len(shape))
    return pl.pallas_call(
        _mix_in_kernel,
        grid=(B, S // tile),
        in_specs=[row(D_MODEL), full((1, D_MODEL)), full((D_MODEL, D_IN)),
                  row(LANES), row(LANES), row(LANES)],
        out_specs=[row(D_ATTN), row(D_ATTN), row(D_ATTN), row(D_CONV)],
        out_shape=[jax.ShapeDtypeStruct((B, S, D_ATTN), BF16)] * 3
        + [jax.ShapeDtypeStruct((B, S, D_CONV), F32)],
        compiler_params=_params(("parallel", "parallel")),
        name="mix_in",
    )(x, g, w_in, cos_t, sa_t, sb_t)


def _attn_kernel(*refs, span, first, tq):
    if first:
        q_ref, kc_ref, vc_ref, kp_ref, vp_ref, o_ref, lse_ref = refs
    else:
        q_ref, kc_ref, vc_ref, kp_ref, vp_ref, oin_ref, lin_ref, o_ref, lse_ref = refs
    n = pl.program_id(2)
    blk = ATTN_BLOCK
    lane = lax.broadcasted_iota(jnp.int32, (1, LANES), 1)
    head0 = lane < HEAD_DIM
    scale = HEAD_DIM ** -0.5
    qscale = [jnp.where(head0, scale, 0.0).astype(BF16), jnp.where(head0, 0.0, scale).astype(BF16)]
    qi = lax.broadcasted_iota(jnp.int32, (blk, 2 * blk), 0)
    kj = lax.broadcasted_iota(jnp.int32, (blk, 2 * blk), 1)
    dist = blk + qi - kj
    band = (dist >= 0) & (dist <= span)
    band_first = band & ((kj >= blk) | (n > 0))

    for qb in range(tq // blk):
        rows = slice(qb * blk, (qb + 1) * blk)
        mask = band_first if qb == 0 else band
        for hp in range(D_ATTN // LANES):
            cols = slice(hp * LANES, (hp + 1) * LANES)
            q2 = q_ref[rows, cols]
            if qb == 0:
                k2 = jnp.concatenate([kp_ref[:, cols], kc_ref[0:blk, cols]], axis=0)
                v2 = jnp.concatenate([vp_ref[:, cols], vc_ref[0:blk, cols]], axis=0)
            else:
                k2 = kc_ref[(qb - 1) * blk:(qb + 1) * blk, cols]
                v2 = vc_ref[(qb - 1) * blk:(qb + 1) * blk, cols]
            o_h, lse_h = [], []
            for hh in range(HEADS_PER_LANE_TILE):
                s = lax.dot_general(q2 * qscale[hh], k2, (((1,), (1,)), ((), ())),
                                    preferred_element_type=F32)
                s = jnp.where(mask, s, NEG_INF)
                m = jnp.max(s, axis=-1, keepdims=True)
                e = jnp.exp(s - m)
                l = jnp.sum(e, axis=-1, keepdims=True)
                o_h.append(jnp.dot(e.astype(BF16), v2, preferred_element_type=F32) / l)
                lse_h.append(m + jnp.log(l))
            o_new = jnp.where(head0, o_h[0], o_h[1])
            lse_new = jnp.where(head0, lse_h[0], lse_h[1])
            if first:
                o_ref[rows, cols] = o_new
                lse_ref[rows, cols] = lse_new
            else:
                lse_old = lin_ref[rows, cols]
                mx = jnp.maximum(lse_old, lse_new)
                w_old = jnp.exp(lse_old - mx)
                w_new = jnp.exp(lse_new - mx)
                tot = w_old + w_new
                o_ref[rows, cols] = (oin_ref[rows, cols] * w_old + o_new * w_new) / tot
                lse_ref[rows, cols] = mx + jnp.log(tot)


def _attn_pattern(q, k, v, acc, window, dilation, tq):
    B, S, _ = q.shape
    L = S // dilation
    tq = min(tq, L)
    nprev = tq // ATTN_BLOCK
    view = lambda t: t.reshape(B, L, dilation * D_ATTN)
    cur = pl.BlockSpec((None, tq, D_ATTN), lambda b, r, n: (b, n, r))
    prev = pl.BlockSpec((None, ATTN_BLOCK, D_ATTN),
                        lambda b, r, n: (b, jnp.maximum(n * nprev - 1, 0), r))
    first = acc is None
    ins = [view(q), view(k), view(v), view(k), view(v)]
    in_specs = [cur, cur, cur, prev, prev]
    aliases = {}
    if not first:
        ins += [view(acc[0]), view(acc[1])]
        in_specs += [cur, cur]
        aliases = {5: 0, 6: 1}
    o, lse = pl.pallas_call(
        functools.partial(_attn_kernel, span=window // dilation, first=first, tq=tq),
        grid=(B, dilation, L // tq),
        in_specs=in_specs,
        out_specs=[cur, cur],
        out_shape=[jax.ShapeDtypeStruct((B, L, dilation * D_ATTN), F32)] * 2,
        input_output_aliases=aliases,
        compiler_params=_params(("parallel", "parallel", "parallel")),
        name=f"attn_d{dilation}",
    )(*ins)
    return o.reshape(B, S, D_ATTN), lse.reshape(B, S, D_ATTN)


def _mix_out_kernel(x_ref, attn_ref, glu_ref, halo_ref, cw_ref, cb_ref, lng_ref, lnb_ref,
                    ga_ref, gc_ref, wo_ref, gffn_ref, h_ref, xnt_ref, win_ref, *, tile):
    i = pl.program_id(1)
    halo = halo_ref[...]
    win_ref[0:CONV_HALO, :] = jnp.where(i > 0, halo, jnp.zeros_like(halo))
    win_ref[CONV_HALO:, :] = glu_ref[...]
    off = CONV_HALO - (CONV_WIDTH - 1)
    c = jnp.zeros((tile, D_CONV), F32)
    for j in range(CONV_WIDTH):
        c = c + win_ref[pl.ds(off + j, tile), :] * cw_ref[j:j + 1, :]
    c = c + cb_ref[...]
    mu = jnp.mean(c, axis=-1, keepdims=True)
    var = jnp.mean(jnp.square(c - mu), axis=-1, keepdims=True)
    c = (c - mu) * lax.rsqrt(var + EPS) * lng_ref[...] + lnb_ref[...]
    c = c * jax.nn.sigmoid(c)
    y = jnp.dot(_rms(attn_ref[...], ga_ref[...]).astype(BF16), wo_ref[0:D_ATTN, :],
                preferred_element_type=F32)
    y = y + jnp.dot(_rms(c, gc_ref[...]).astype(BF16), wo_ref[D_ATTN:, :],
                    preferred_element_type=F32)
    h = x_ref[...] + y
    h_ref[...] = h
    xnt_ref[...] = _rms(h, gffn_ref[...]).T.astype(BF16)


def _mix_out(x, attn, glu, conv_w, conv_b, ln_g, ln_b, g_attn, g_conv, w_out, g_ffn, tile):
    B, S, _ = x.shape
    nt = S // tile
    hpt = tile // CONV_HALO
    row = lambda d: pl.BlockSpec((None, tile, d), lambda b, i: (b, i, 0))
    full = lambda shape: pl.BlockSpec(shape, lambda b, i: (0,) * len(shape))
    halo = pl.BlockSpec((None, CONV_HALO, D_CONV), lambda b, i: (b, jnp.maximum(i * hpt - 1, 0), 0))
    return pl.pallas_call(
        functools.partial(_mix_out_kernel, tile=tile),
        grid=(B, nt),
        in_specs=[row(D_MODEL), row(D_ATTN), row(D_CONV), halo,
                  full((CONV_WIDTH, D_CONV)), full((1, D_CONV)), full((1, D_CONV)), full((1, D_CONV)),
                  full((1, D_ATTN)), full((1, D_CONV)), full((D_MODEL, D_MODEL)), full((1, D_MODEL))],
        out_specs=[row(D_MODEL), pl.BlockSpec((D_MODEL, tile), lambda b, i: (0, b * nt + i))],
        out_shape=[jax.ShapeDtypeStruct((B, S, D_MODEL), F32),
                   jax.ShapeDtypeStruct((D_MODEL, B * S), BF16)],
        scratch_shapes=[pltpu.VMEM((CONV_HALO + tile, D_CONV), F32)],
        compiler_params=_params(("parallel", "parallel")),
        name="mix_out",
    )(x, attn, glu, glu, conv_w, conv_b, ln_g, ln_b, g_attn, g_conv, w_out, g_ffn)


def _top16_rows(s, out_ref):
    for j in range(PEER_TOPK):
        m = jnp.max(s, axis=0, keepdims=True)
        out_ref[j:j + 1, :] = m
        s = jnp.where(s >= m, NEG_INF, s)


def _peer_select_kernel(xt_ref, wq_ref, keys_ref, w1_ref, c_ref, w2_ref, r2_ref,
                        q_ref, a_ref, b_ref, cand_ref):
    q_ref[...] = jnp.dot(wq_ref[...], xt_ref[...], preferred_element_type=F32).astype(BF16)
    cand_ref[...] = jnp.full(cand_ref.shape, NEG_INF, F32)

    def head(h, carry):
        base = pl.multiple_of(h * 2 * N_KEYS, 2 * N_KEYS)
        s1 = jnp.dot(keys_ref[0], q_ref[pl.ds(base, N_KEYS), :], preferred_element_type=F32)
        s2 = jnp.dot(keys_ref[1], q_ref[pl.ds(base + N_KEYS, N_KEYS), :], preferred_element_type=F32)
        _top16_rows(s1, a_ref)
        _top16_rows(s2, b_ref)
        for idx, (j, k) in enumerate(_STAIR):
            cand_ref[idx:idx + 1, :] = a_ref[j:j + 1, :] + b_ref[k:k + 1, :]
        cand = cand_ref[...]
        work = cand
        for _ in range(PEER_TOPK):
            tau = jnp.max(work, axis=0, keepdims=True)
            work = jnp.where(work >= tau, NEG_INF, work)
        top = a_ref[0:1, :] + b_ref[0:1, :]
        z = jnp.sum(jnp.where(cand >= tau, jnp.exp(cand - top), 0.0), axis=0, keepdims=True)
        cnt = jnp.zeros(s1.shape, F32)
        rank = jnp.zeros(s2.shape, F32)
        for k in range(PEER_TOPK):
            bk = b_ref[k:k + 1, :]
            cnt = cnt + jnp.where(s1 + bk >= tau, 1.0, 0.0)
            rank = rank + jnp.where(bk > s2, 1.0, 0.0)
        w1_ref[h] = jnp.exp(s1 - a_ref[0:1, :]) / z
        c_ref[h] = cnt
        w2_ref[h] = jnp.exp(s2 - b_ref[0:1, :]).astype(BF16)
        r2_ref[h] = rank.astype(BF16)
        return carry

    lax.fori_loop(0, PEER_HEADS, head, 0)


def _peer_select(xnt, wq_t, keys, tile):
    n = xnt.shape[1]
    tab = pl.BlockSpec((PEER_HEADS, N_KEYS, tile), lambda i: (0, 0, i))
    return pl.pallas_call(
        _peer_select_kernel,
        grid=(n // tile,),
        in_specs=[pl.BlockSpec((D_MODEL, tile), lambda i: (0, i)),
                  pl.BlockSpec(wq_t.shape, lambda i: (0, 0)),
                  pl.BlockSpec(keys.shape, lambda i: (0, 0, 0))],
        out_specs=[tab, tab, tab, tab],
        out_shape=[jax.ShapeDtypeStruct((PEER_HEADS, N_KEYS, n), F32)] * 2
        + [jax.ShapeDtypeStruct((PEER_HEADS, N_KEYS, n), BF16)] * 2,
        scratch_shapes=[pltpu.VMEM((PEER_HEADS * 2 * N_KEYS, tile), BF16),
                        pltpu.VMEM((PEER_TOPK, tile), F32),
                        pltpu.VMEM((PEER_TOPK, tile), F32),
                        pltpu.VMEM((_STAIR_ROWS, tile), F32)],
        compiler_params=_params(("parallel",)),
        name="peer_select",
    )(xnt, wq_t, keys)


def _peer_dense_kernel(xt_ref, u_ref, vt_ref, w1_ref, c_ref, w2_ref, r2_ref, h_ref, o_ref, acc_ref,
                       *, chunk, sub):
    j = pl.program_id(1)

    @pl.when(j == 0)
    def _():
        acc_ref[...] = jnp.zeros_like(acc_ref)

    xt = xt_ref[...]
    tile = xt.shape[1]
    per_sub = sub // N_KEYS
    for sb in range(chunk // sub):
        gates = []
        for ii in range(per_sub):
            row = sb * per_sub + ii
            g = jnp.zeros((N_KEYS, tile), BF16)
            for h in range(PEER_HEADS):
                cnt = jnp.broadcast_to(c_ref[h, row:row + 1, :], (N_KEYS, tile)).astype(BF16)
                w1 = jnp.broadcast_to(w1_ref[h, row:row + 1, :], (N_KEYS, tile)).astype(BF16)
                g = g + jnp.where(r2_ref[h] < cnt, w2_ref[h], jnp.zeros_like(g)) * w1
            gates.append(g)
        gate = jnp.concatenate(gates, axis=0) if per_sub > 1 else gates[0]
        hid = jnp.dot(u_ref[sb * sub:(sb + 1) * sub, :], xt, preferred_element_type=F32)
        gelu = 0.5 * hid * (1.0 + lax.erf(hid * math.sqrt(0.5)))
        act = gelu.astype(BF16) * gate
        acc_ref[...] += jnp.dot(vt_ref[:, sb * sub:(sb + 1) * sub], act, preferred_element_type=F32)

    @pl.when(j == pl.num_programs(1) - 1)
    def _():
        o_ref[...] = h_ref[...] + acc_ref[...].T


def _peer_dense(xnt, u, vt, w1, cnt, w2, r2, h, tile, chunk, sub):
    n = xnt.shape[1]
    rows_per_chunk = chunk // N_KEYS
    small = pl.BlockSpec((PEER_HEADS, rows_per_chunk, tile), lambda i, j: (0, j, i))
    big = pl.BlockSpec((PEER_HEADS, N_KEYS, tile), lambda i, j: (0, 0, i))
    return pl.pallas_call(
        functools.partial(_peer_dense_kernel, chunk=chunk, sub=sub),
        grid=(n // tile, N_EXPERTS // chunk),
        in_specs=[pl.BlockSpec((D_MODEL, tile), lambda i, j: (0, i)),
                  pl.BlockSpec((chunk, D_MODEL), lambda i, j: (j, 0)),
                  pl.BlockSpec((D_MODEL, chunk), lambda i, j: (0, j)),
                  small, small, big, big,
                  pl.BlockSpec((tile, D_MODEL), lambda i, j: (i, 0))],
        out_specs=pl.BlockSpec((tile, D_MODEL), lambda i, j: (i, 0)),
        out_shape=jax.ShapeDtypeStruct((n, D_MODEL), F32),
        scratch_shapes=[pltpu.VMEM((D_MODEL, tile), F32)],
        compiler_params=_params(("parallel", "arbitrary")),
        name="peer_dense",
    )(xnt, u, vt, w1, cnt, w2, r2, h)


def _ple_final_kernel(h_ref, p_ref, gple_ref, wg_ref, wp_ref, gfin_ref, o_ref, *, last):
    h = h_ref[...]
    gate = jax.nn.sigmoid(jnp.dot(_rms(h, gple_ref[...]).astype(BF16), wg_ref[...],
                                  preferred_element_type=F32))
    proj = jnp.dot(p_ref[...].astype(BF16), wp_ref[...], preferred_element_type=F32)
    h = h + gate * proj
    o_ref[...] = _rms(h, gfin_ref[...]) if last else h


def _ple_final(h, p, g_ple, w_gate, w_proj, g_final, tile, last):
    n = h.shape[0]
    full = lambda shape: pl.BlockSpec(shape, lambda i: (0,) * len(shape))
    return pl.pallas_call(
        functools.partial(_ple_final_kernel, last=last),
        grid=(n // tile,),
        in_specs=[pl.BlockSpec((tile, D_MODEL), lambda i: (i, 0)),
                  pl.BlockSpec((tile, PLE_DIM), lambda i: (i, 0)),
                  full((1, D_MODEL)), full((D_MODEL, D_MODEL)), full((PLE_DIM, D_MODEL)),
                  full((1, D_MODEL))],
        out_specs=pl.BlockSpec((tile, D_MODEL), lambda i: (i, 0)),
        out_shape=jax.ShapeDtypeStruct((n, D_MODEL), F32),
        compiler_params=_params(("parallel",)),
        name="ple_final",
    )(h, p, g_ple, w_gate, w_proj, g_final)


def _rope_tables(positions):
    half = ROPE_DIM // 2
    inv_freq = ROPE_THETA ** (-jnp.arange(0, ROPE_DIM, 2, dtype=F32) / ROPE_DIM)
    ang = positions.astype(F32)[..., None] * inv_freq
    cos, sin = jnp.cos(ang), jnp.sin(ang)
    ones = jnp.ones(ang.shape[:-1] + (HEAD_DIM - ROPE_DIM,), F32)
    zeros = jnp.zeros_like(ones)
    zh = jnp.zeros_like(sin)
    tile = lambda parts: jnp.concatenate(parts * HEADS_PER_LANE_TILE, axis=-1)
    return tile([cos, cos, ones]), tile([-sin, zh, zeros]), tile([zh, sin, zeros])


def kernel(x, p, positions, norm_mix, w_in, conv_w, conv_b, conv_ln_g, conv_ln_b, g_attn_out,
           g_conv_out, w_out, norm_ffn, peer_wq, sub_keys, expert_u, expert_v, norm_ple,
           w_ple_gate, w_ple_proj, final_norm):
    B, S, _ = x.shape
    depth = w_in.shape[0]
    n = B * S
    cos_t, sa_t, sb_t = _rope_tables(positions)
    vec = lambda t: t.reshape(1, -1)
    h = x
    for i in range(depth):
        q, k, v, glu = _mix_in(h, vec(norm_mix[i]), w_in[i].astype(BF16), cos_t, sa_t, sb_t, tile=512)
        acc = None
        for window, dilation in DILATED_PATTERNS:
            acc = _attn_pattern(q, k, v, acc, window, dilation, tq=512)
        h, xnt = _mix_out(h, acc[0], glu, conv_w[i], vec(conv_b[i]), vec(conv_ln_g[i]),
                          vec(conv_ln_b[i]), vec(g_attn_out[i]), vec(g_conv_out[i]),
                          w_out[i].astype(BF16), vec(norm_ffn[i]), tile=256)
        w1, cnt, w2, r2 = _peer_select(xnt, peer_wq[i].T.astype(BF16), sub_keys[i].astype(BF16),
                                       tile=256)
        h = _peer_dense(xnt, expert_u[i].astype(BF16), expert_v[i].T.astype(BF16), w1, cnt, w2, r2,
                        h.reshape(n, D_MODEL), tile=512, chunk=1024, sub=256)
        h = _ple_final(h, p[i].reshape(n, PLE_DIM), vec(norm_ple[i]), w_ple_gate[i].astype(BF16),
                       w_ple_proj[i].astype(BF16), vec(final_norm), tile=512, last=(i == depth - 1))
        h = h.reshape(B, S, D_MODEL)
    return h
```

```python
import functools
import math

import jax
import jax.numpy as jnp
from jax import lax
from jax.experimental import pallas as pl
from jax.experimental.pallas import tpu as pltpu

F32 = jnp.float32
BF16 = jnp.bfloat16

D_MODEL = 1024
PLE_DIM = 256
HEAD_DIM = 64
N_ATTN_HEADS = 8
D_ATTN = N_ATTN_HEADS * HEAD_DIM
D_CONV = D_MODEL // 2
D_IN = 3 * D_ATTN + 2 * D_CONV
CONV_WIDTH = 31
ROPE_DIM = HEAD_DIM // 4
ROPE_THETA = 500000.0
DILATED_PATTERNS = ((128, 1), (512, 4), (2048, 16))
ATTN_BLOCK = 128
N_KEYS = 128
N_EXPERTS = N_KEYS * N_KEYS
PEER_HEADS = 8
PEER_TOPK = 16
EPS = 1e-6

LANES = 128
HEADS_PER_LANE_TILE = LANES // HEAD_DIM
CONV_HALO = 32
VMEM_LIMIT = 56 * 1024 * 1024

_STAIR = [(j, k) for j in range(PEER_TOPK) for k in range(PEER_TOPK) if (j + 1) * (k + 1) <= PEER_TOPK]
_STAIR_ROWS = -(-len(_STAIR) // 8) * 8

NEG_INF = float("-inf")


def _params(sem):
    return pltpu.CompilerParams(dimension_semantics=sem, vmem_limit_bytes=VMEM_LIMIT)


def _rms(x, g):
    return x * lax.rsqrt(jnp.mean(x * x, axis=-1, keepdims=True) + EPS) * g


def _mix_in_kernel(x_ref, g_ref, w_ref, cos_ref, sin_ref, q_ref, k_ref, v_ref, glu_ref):
    xn = _rms(x_ref[...], g_ref[...])
    z = jnp.dot(xn.astype(BF16), w_ref[...], preferred_element_type=F32)
    half = ROPE_DIM // 2
    cos, sin = cos_ref[...], sin_ref[...]
    dim = lax.broadcasted_iota(jnp.int32, (1, LANES), 1) % HEAD_DIM
    sa = jnp.where(dim < half, -sin, 0.0)
    sb = jnp.where((dim >= half) & (dim < ROPE_DIM), sin, 0.0)

    def rope(tt):
        return tt * cos + pltpu.roll(tt, LANES - half, 1) * sa + pltpu.roll(tt, half, 1) * sb

    for i in range(D_ATTN // LANES):
        q_ref[i] = rope(z[:, i * LANES:(i + 1) * LANES])
        k_ref[i] = rope(z[:, D_ATTN + i * LANES:D_ATTN + (i + 1) * LANES])
        v_ref[i] = z[:, 2 * D_ATTN + i * LANES:2 * D_ATTN + (i + 1) * LANES]
    a = z[:, 3 * D_ATTN:3 * D_ATTN + D_CONV]
    gt = z[:, 3 * D_ATTN + D_CONV:]
    glu_ref[...] = a * jax.nn.sigmoid(gt)


def _mix_in(x, g, w_in, cos_t, sin_t, tile):
    B, S, _ = x.shape
    n_slab = D_ATTN // LANES
    row = lambda d: pl.BlockSpec((None, tile, d), lambda b, i: (b, i, 0))
    full = lambda shape: pl.BlockSpec(shape, lambda b, i: (0,) * len(shape))
    slab = pl.BlockSpec((None, n_slab, tile, LANES), lambda b, i: (b, 0, i, 0))
    return pl.pallas_call(
        _mix_in_kernel,
        grid=(B, S // tile),
        in_specs=[row(D_MODEL), full((1, D_MODEL)), full((D_MODEL, D_IN)), row(LANES), row(LANES)],
        out_specs=[slab, slab, slab, row(D_CONV)],
        out_shape=[jax.ShapeDtypeStruct((B, n_slab, S, LANES), F32)] * 3
        + [jax.ShapeDtypeStruct((B, S, D_CONV), F32)],
        compiler_params=_params(("parallel", "parallel")),
        name="mix_in",
    )(x, g, w_in, cos_t, sin_t)


def _attn_kernel(*refs, span, dil, first):
    if first:
        q_ref, kc_ref, vc_ref, kp_ref, vp_ref, o_ref, lse_ref = refs
    else:
        q_ref, kc_ref, vc_ref, kp_ref, vp_ref, oin_ref, lin_ref, o_ref, lse_ref = refs
    n = pl.program_id(2)
    blk = ATTN_BLOCK

    def rows(start, count):
        return pl.ds(start, count, stride=dil) if dil > 1 else pl.ds(start, count)

    lane = lax.broadcasted_iota(jnp.int32, (1, LANES), 1)
    head0 = lane < HEAD_DIM
    scale = HEAD_DIM ** -0.5
    qscale = [jnp.where(head0, scale, 0.0).astype(BF16), jnp.where(head0, 0.0, scale).astype(BF16)]
    qi = lax.broadcasted_iota(jnp.int32, (blk, 2 * blk), 0)
    kj = lax.broadcasted_iota(jnp.int32, (blk, 2 * blk), 1)
    dist = blk + qi - kj
    band = (dist >= 0) & (dist <= span)
    band_first = band & ((kj >= blk) | (n > 0))

    for r in range(dil):
        for qb in range(q_ref.shape[0] // (blk * dil)):
            mask = band_first if qb == 0 else band
            qrows = rows(qb * blk * dil + r, blk)
            q2 = q_ref[qrows, :].astype(BF16)
            if qb == 0:
                k2 = jnp.concatenate([kp_ref[rows(r, blk), :], kc_ref[rows(r, blk), :]], axis=0)
                v2 = jnp.concatenate([vp_ref[rows(r, blk), :], vc_ref[rows(r, blk), :]], axis=0)
            else:
                krows = rows((qb - 1) * blk * dil + r, 2 * blk)
                k2, v2 = kc_ref[krows, :], vc_ref[krows, :]
            k2, v2 = k2.astype(BF16), v2.astype(BF16)
            o_h, lse_h = [], []
            for hh in range(HEADS_PER_LANE_TILE):
                s = lax.dot_general(q2 * qscale[hh], k2, (((1,), (1,)), ((), ())),
                                    preferred_element_type=F32)
                s = jnp.where(mask, s, NEG_INF)
                m = jnp.max(s, axis=-1, keepdims=True)
                e = jnp.exp(s - m)
                l = jnp.sum(e, axis=-1, keepdims=True)
                o_h.append(jnp.dot(e.astype(BF16), v2, preferred_element_type=F32) / l)
                lse_h.append(m + jnp.log(l))
            o_new = jnp.where(head0, o_h[0], o_h[1])
            lse_new = jnp.where(head0, lse_h[0], lse_h[1])
            if first:
                o_ref[qrows, :] = o_new
                lse_ref[qrows, :] = lse_new
            else:
                lse_old = lin_ref[qrows, :]
                mx = jnp.maximum(lse_old, lse_new)
                w_old = jnp.exp(lse_old - mx)
                w_new = jnp.exp(lse_new - mx)
                tot = w_old + w_new
                o_ref[qrows, :] = (oin_ref[qrows, :] * w_old + o_new * w_new) / tot
                lse_ref[qrows, :] = mx + jnp.log(tot)


def _attn_pattern(q, k, v, acc, window, dilation, super_block):
    B, n_slab, S, _ = q.shape
    reach = ATTN_BLOCK * dilation
    assert super_block % reach == 0 and S % super_block == 0
    cur = pl.BlockSpec((None, None, super_block, LANES), lambda b, hp, n: (b, hp, n, 0))
    prev = pl.BlockSpec((None, None, reach, LANES),
                        lambda b, hp, n: (b, hp, jnp.maximum(n * (super_block // reach) - 1, 0), 0))
    first = acc is None
    ins = [q, k, v, k, v]
    in_specs = [cur, cur, cur, prev, prev]
    aliases = {}
    if not first:
        ins += list(acc)
        in_specs += [cur, cur]
        aliases = {5: 0, 6: 1}
    return pl.pallas_call(
        functools.partial(_attn_kernel, span=window // dilation, dil=dilation, first=first),
        grid=(B, n_slab, S // super_block),
        in_specs=in_specs,
        out_specs=[cur, cur],
        out_shape=[jax.ShapeDtypeStruct(q.shape, F32)] * 2,
        input_output_aliases=aliases,
        compiler_params=_params(("parallel", "parallel", "parallel")),
        name=f"attn_d{dilation}",
    )(*ins)


def _mix_out_kernel(x_ref, attn_ref, glu_ref, halo_ref, cw_ref, cb_ref, lng_ref, lnb_ref,
                    ga_ref, gc_ref, wo_ref, gffn_ref, h_ref, xnt_ref, win_ref, *, tile):
    i = pl.program_id(1)
    halo = halo_ref[...]
    win_ref[0:CONV_HALO, :] = jnp.where(i > 0, halo, jnp.zeros_like(halo))
    win_ref[CONV_HALO:, :] = glu_ref[...]
    off = CONV_HALO - (CONV_WIDTH - 1)
    c = jnp.zeros((tile, D_CONV), F32)
    for j in range(CONV_WIDTH):
        c = c + win_ref[pl.ds(off + j, tile), :] * cw_ref[j:j + 1, :]
    c = c + cb_ref[...]
    mu = jnp.mean(c, axis=-1, keepdims=True)
    var = jnp.mean(jnp.square(c - mu), axis=-1, keepdims=True)
    c = (c - mu) * lax.rsqrt(var + EPS) * lng_ref[...] + lnb_ref[...]
    c = c * jax.nn.sigmoid(c)
    attn = jnp.concatenate([attn_ref[s] for s in range(D_ATTN // LANES)], axis=1)
    y = jnp.dot(_rms(attn, ga_ref[...]).astype(BF16), wo_ref[0:D_ATTN, :],
                preferred_element_type=F32)
    y = y + jnp.dot(_rms(c, gc_ref[...]).astype(BF16), wo_ref[D_ATTN:, :],
                    preferred_element_type=F32)
    h = x_ref[...] + y
    h_ref[...] = h
    xnt_ref[...] = _rms(h, gffn_ref[...]).T.astype(BF16)


def _mix_out(x, attn, glu, conv_w, conv_b, ln_g, ln_b, g_attn, g_conv, w_out, g_ffn, tile):
    B, S, _ = x.shape
    nt = S // tile
    hpt = tile // CONV_HALO
    row = lambda d: pl.BlockSpec((None, tile, d), lambda b, i: (b, i, 0))
    full = lambda shape: pl.BlockSpec(shape, lambda b, i: (0,) * len(shape))
    halo = pl.BlockSpec((None, CONV_HALO, D_CONV), lambda b, i: (b, jnp.maximum(i * hpt - 1, 0), 0))
    return pl.pallas_call(
        functools.partial(_mix_out_kernel, tile=tile),
        grid=(B, nt),
        in_specs=[row(D_MODEL),
                  pl.BlockSpec((None, D_ATTN // LANES, tile, LANES), lambda b, i: (b, 0, i, 0)),
                  row(D_CONV), halo,
                  full((CONV_WIDTH, D_CONV)), full((1, D_CONV)), full((1, D_CONV)), full((1, D_CONV)),
                  full((1, D_ATTN)), full((1, D_CONV)), full((D_MODEL, D_MODEL)), full((1, D_MODEL))],
        out_specs=[row(D_MODEL), pl.BlockSpec((D_MODEL, tile), lambda b, i: (0, b * nt + i))],
        out_shape=[jax.ShapeDtypeStruct((B, S, D_MODEL), F32),
                   jax.ShapeDtypeStruct((D_MODEL, B * S), BF16)],
        scratch_shapes=[pltpu.VMEM((CONV_HALO + tile, D_CONV), F32)],
        compiler_params=_params(("parallel", "parallel")),
        name="mix_out",
    )(x, attn, glu, glu, conv_w, conv_b, ln_g, ln_b, g_attn, g_conv, w_out, g_ffn)


def _top16_rows(s, out_ref):
    for j in range(PEER_TOPK):
        m = jnp.max(s, axis=0, keepdims=True)
        out_ref[j:j + 1, :] = m
        s = jnp.where(s >= m, NEG_INF, s)


def _peer_select_kernel(xt_ref, wq_ref, keys_ref, w1_ref, c_ref, w2_ref, r2_ref,
                        q_ref, a_ref, b_ref, cand_ref):
    q_ref[...] = jnp.dot(wq_ref[...], xt_ref[...], preferred_element_type=F32).astype(BF16)
    cand_ref[...] = jnp.full(cand_ref.shape, NEG_INF, F32)

    def head(h, carry):
        base = pl.multiple_of(h * 2 * N_KEYS, 2 * N_KEYS)
        s1 = jnp.dot(keys_ref[0], q_ref[pl.ds(base, N_KEYS), :], preferred_element_type=F32)
        s2 = jnp.dot(keys_ref[1], q_ref[pl.ds(base + N_KEYS, N_KEYS), :], preferred_element_type=F32)
        _top16_rows(s1, a_ref)
        _top16_rows(s2, b_ref)
        for idx, (j, k) in enumerate(_STAIR):
            cand_ref[idx:idx + 1, :] = a_ref[j:j + 1, :] + b_ref[k:k + 1, :]
        cand = cand_ref[...]
        work = cand
        for _ in range(PEER_TOPK):
            tau = jnp.max(work, axis=0, keepdims=True)
            work = jnp.where(work >= tau, NEG_INF, work)
        top = a_ref[0:1, :] + b_ref[0:1, :]
        z = jnp.sum(jnp.where(cand >= tau, jnp.exp(cand - top), 0.0), axis=0, keepdims=True)
        cnt = jnp.zeros(s1.shape, F32)
        rank = jnp.zeros(s2.shape, F32)
        for k in range(PEER_TOPK):
            bk = b_ref[k:k + 1, :]
            cnt = cnt + jnp.where(s1 + bk >= tau, 1.0, 0.0)
            rank = rank + jnp.where(bk > s2, 1.0, 0.0)
        w1_ref[h] = jnp.exp(s1 - a_ref[0:1, :]) / z
        c_ref[h] = cnt
        w2_ref[h] = jnp.exp(s2 - b_ref[0:1, :]).astype(BF16)
        r2_ref[h] = rank.astype(BF16)
        return carry

    lax.fori_loop(0, PEER_HEADS, head, 0)


def _peer_select(xnt, wq_t, keys, tile):
    n = xnt.shape[1]
    tab = pl.BlockSpec((PEER_HEADS, N_KEYS, tile), lambda i: (0, 0, i))
    return pl.pallas_call(
        _peer_select_kernel,
        grid=(n // tile,),
        in_specs=[pl.BlockSpec((D_MODEL, tile), lambda i: (0, i)),
                  pl.BlockSpec(wq_t.shape, lambda i: (0, 0)),
                  pl.BlockSpec(keys.shape, lambda i: (0, 0, 0))],
        out_specs=[tab, tab, tab, tab],
        out_shape=[jax.ShapeDtypeStruct((PEER_HEADS, N_KEYS, n), F32)] * 2
        + [jax.ShapeDtypeStruct((PEER_HEADS, N_KEYS, n), BF16)] * 2,
        scratch_shapes=[pltpu.VMEM((PEER_HEADS * 2 * N_KEYS, tile), BF16),
                        pltpu.VMEM((PEER_TOPK, tile), F32),
                        pltpu.VMEM((PEER_TOPK, tile), F32),
                        pltpu.VMEM((_STAIR_ROWS, tile), F32)],
        compiler_params=_params(("parallel",)),
        name="peer_select",
    )(xnt, wq_t, keys)


SUB = 8
PAIR = 2 * LANES


def _peer_dense_kernel(xt_ref, u_ref, vt_ref, w1_ref, c_ref, w2_ref, r2_ref, h_ref, o_ref,
                       hid_ref, act_ref, acc_ref):
    j = pl.program_id(1)
    tile = xt_ref.shape[1]
    rows_per_chunk = u_ref.shape[0] // N_KEYS

    @pl.when(j == 0)
    def _():
        acc_ref[...] = jnp.zeros_like(acc_ref)

    hid_ref[...] = jnp.dot(u_ref[...], xt_ref[...], preferred_element_type=F32)
    for row in range(rows_per_chunk):
        for lp in range(tile // PAIR):
            lanes = slice(lp * PAIR, (lp + 1) * PAIR)

            def bcast(ref, h):
                parts = [jnp.broadcast_to(ref[h, row:row + 1, lp * PAIR + k * LANES:lp * PAIR + (k + 1) * LANES],
                                          (SUB, LANES)) for k in range(PAIR // LANES)]
                return jnp.concatenate(parts, axis=1).astype(BF16)

            gate = [jnp.zeros((SUB, PAIR), BF16) for _ in range(N_KEYS // SUB)]
            for h in range(PEER_HEADS):
                cnt = bcast(c_ref, h)
                w1 = bcast(w1_ref, h)
                for v in range(N_KEYS // SUB):
                    keys = slice(v * SUB, (v + 1) * SUB)
                    sel = jnp.where(r2_ref[h, keys, lanes] < cnt, w2_ref[h, keys, lanes],
                                    jnp.zeros((SUB, PAIR), BF16))
                    gate[v] = gate[v] + sel * w1
            for v in range(N_KEYS // SUB):
                rows = slice(row * N_KEYS + v * SUB, row * N_KEYS + (v + 1) * SUB)
                hid = hid_ref[rows, lanes]
                gelu = 0.5 * hid * (1.0 + lax.erf(hid * math.sqrt(0.5)))
                act_ref[rows, lanes] = gelu.astype(BF16) * gate[v]
    acc_ref[...] += jnp.dot(vt_ref[...], act_ref[...], preferred_element_type=F32)

    @pl.when(j == pl.num_programs(1) - 1)
    def _():
        o_ref[...] = h_ref[...] + acc_ref[...].T


def _peer_dense(xnt, u, vt, w1, cnt, w2, r2, h, tile, chunk):
    n = xnt.shape[1]
    rows_per_chunk = chunk // N_KEYS
    small = pl.BlockSpec((PEER_HEADS, rows_per_chunk, tile), lambda i, j: (0, j, i))
    big = pl.BlockSpec((PEER_HEADS, N_KEYS, tile), lambda i, j: (0, 0, i))
    res = pl.BlockSpec((tile, D_MODEL), lambda i, j: (i, 0))
    return pl.pallas_call(
        _peer_dense_kernel,
        grid=(n // tile, N_EXPERTS // chunk),
        in_specs=[pl.BlockSpec((D_MODEL, tile), lambda i, j: (0, i)),
                  pl.BlockSpec((chunk, D_MODEL), lambda i, j: (j, 0)),
                  pl.BlockSpec((D_MODEL, chunk), lambda i, j: (0, j)),
                  small, small, big, big, res],
        out_specs=res,
        out_shape=jax.ShapeDtypeStruct((n, D_MODEL), F32),
        scratch_shapes=[pltpu.VMEM((chunk, tile), F32), pltpu.VMEM((chunk, tile), BF16),
                        pltpu.VMEM((D_MODEL, tile), F32)],
        compiler_params=_params(("parallel", "arbitrary")),
        name="peer_dense",
    )(xnt, u, vt, w1, cnt, w2, r2, h)


def _ple_final_kernel(h_ref, p_ref, gple_ref, wg_ref, wp_ref, gfin_ref, o_ref, *, last):
    h = h_ref[...]
    gate = jax.nn.sigmoid(jnp.dot(_rms(h, gple_ref[...]).astype(BF16), wg_ref[...],
                                  preferred_element_type=F32))
    proj = jnp.dot(p_ref[...].astype(BF16), wp_ref[...], preferred_element_type=F32)
    h = h + gate * proj
    o_ref[...] = _rms(h, gfin_ref[...]) if last else h


def _ple_final(h, p, g_ple, w_gate, w_proj, g_final, tile, last):
    n = h.shape[0]
    full = lambda shape: pl.BlockSpec(shape, lambda i: (0,) * len(shape))
    return pl.pallas_call(
        functools.partial(_ple_final_kernel, last=last),
        grid=(n // tile,),
        in_specs=[pl.BlockSpec((tile, D_MODEL), lambda i: (i, 0)),
                  pl.BlockSpec((tile, PLE_DIM), lambda i: (i, 0)),
                  full((1, D_MODEL)), full((D_MODEL, D_MODEL)), full((PLE_DIM, D_MODEL)),
                  full((1, D_MODEL))],
        out_specs=pl.BlockSpec((tile, D_MODEL), lambda i: (i, 0)),
        out_shape=jax.ShapeDtypeStruct((n, D_MODEL), F32),
        compiler_params=_params(("parallel",)),
        name="ple_final",
    )(h, p, g_ple, w_gate, w_proj, g_final)


def _rope_tables(positions):
    half = ROPE_DIM // 2
    dim = jnp.arange(LANES) % HEAD_DIM
    inv_freq = jnp.where(dim < ROPE_DIM, ROPE_THETA ** (-2.0 * (dim % half).astype(F32) / ROPE_DIM), 0.0)
    ang = positions.astype(F32)[..., None] * inv_freq
    return jnp.cos(ang), jnp.sin(ang)


def kernel(x, p, positions, norm_mix, w_in, conv_w, conv_b, conv_ln_g, conv_ln_b, g_attn_out,
           g_conv_out, w_out, norm_ffn, peer_wq, sub_keys, expert_u, expert_v, norm_ple,
           w_ple_gate, w_ple_proj, final_norm):
    B, S, _ = x.shape
    depth = w_in.shape[0]
    n = B * S
    cos_t, sin_t = _rope_tables(positions)
    super_block = ATTN_BLOCK * max(d for _, d in DILATED_PATTERNS)
    vec = lambda t: t.reshape(1, -1)
    h = x
    for i in range(depth):
        q, k, v, glu = _mix_in(h, vec(norm_mix[i]), w_in[i].astype(BF16), cos_t, sin_t, tile=512)
        acc = None
        for window, dilation in DILATED_PATTERNS:
            acc = _attn_pattern(q, k, v, acc, window, dilation, super_block)
        h, xnt = _mix_out(h, acc[0], glu, conv_w[i], vec(conv_b[i]), vec(conv_ln_g[i]),
                          vec(conv_ln_b[i]), vec(g_attn_out[i]), vec(g_conv_out[i]),
                          w_out[i].astype(BF16), vec(norm_ffn[i]), tile=256)
        w1, cnt, w2, r2 = _peer_select(xnt, peer_wq[i].T.astype(BF16), sub_keys[i].astype(BF16),
                                       tile=256)
        h = _peer_dense(xnt, expert_u[i].astype(BF16), expert_v[i].T.astype(BF16), w1, cnt, w2, r2,
                        h.reshape(n, D_MODEL), tile=512, chunk=1024)
        h = _ple_final(h, p[i].reshape(n, PLE_DIM), vec(norm_ple[i]), w_ple_gate[i].astype(BF16),
                       w_ple_proj[i].astype(BF16), vec(final_norm), tile=512, last=(i == depth - 1))
        h = h.reshape(B, S, D_MODEL)
    return h
```

```python
import functools
import math

import jax
import jax.numpy as jnp
from jax import lax
from jax.experimental import pallas as pl
from jax.experimental.pallas import tpu as pltpu

F32 = jnp.float32
BF16 = jnp.bfloat16

D_MODEL = 1024
PLE_DIM = 256
HEAD_DIM = 64
N_ATTN_HEADS = 8
D_ATTN = N_ATTN_HEADS * HEAD_DIM
D_CONV = D_MODEL // 2
D_IN = 3 * D_ATTN + 2 * D_CONV
CONV_WIDTH = 31
ROPE_DIM = HEAD_DIM // 4
ROPE_THETA = 500000.0
DILATED_PATTERNS = ((128, 1), (512, 4), (2048, 16))
ATTN_BLOCK = 128
N_KEYS = 128
N_EXPERTS = N_KEYS * N_KEYS
PEER_HEADS = 8
PEER_TOPK = 16
EPS = 1e-6

LANES = 128
SUB = 8
PAIR = 2 * LANES
HEADS_PER_LANE_TILE = LANES // HEAD_DIM
CONV_HALO = 32
VMEM_LIMIT = 56 * 1024 * 1024

NEG_INF = float("-inf")


def _params(sem):
    return pltpu.CompilerParams(dimension_semantics=sem, vmem_limit_bytes=VMEM_LIMIT)


def _rms(x, g):
    return x * lax.rsqrt(jnp.mean(x * x, axis=-1, keepdims=True) + EPS) * g


def _mix_in_kernel(x_ref, g_ref, w_ref, cos_ref, sin_ref, q_ref, k_ref, v_ref, glu_ref):
    xn = _rms(x_ref[...], g_ref[...])
    z = jnp.dot(xn.astype(BF16), w_ref[...], preferred_element_type=F32)
    half = ROPE_DIM // 2
    cos, sin = cos_ref[...], sin_ref[...]
    dim = lax.broadcasted_iota(jnp.int32, (1, LANES), 1) % HEAD_DIM
    sa = jnp.where(dim < half, -sin, 0.0)
    sb = jnp.where((dim >= half) & (dim < ROPE_DIM), sin, 0.0)

    def rope(tt):
        return tt * cos + pltpu.roll(tt, LANES - half, 1) * sa + pltpu.roll(tt, half, 1) * sb

    for i in range(D_ATTN // LANES):
        q_ref[i] = rope(z[:, i * LANES:(i + 1) * LANES])
        k_ref[i] = rope(z[:, D_ATTN + i * LANES:D_ATTN + (i + 1) * LANES])
        v_ref[i] = z[:, 2 * D_ATTN + i * LANES:2 * D_ATTN + (i + 1) * LANES]
    a = z[:, 3 * D_ATTN:3 * D_ATTN + D_CONV]
    gt = z[:, 3 * D_ATTN + D_CONV:]
    glu_ref[...] = a * jax.nn.sigmoid(gt)


def _mix_in(x, g, w_in, cos_t, sin_t, tile):
    B, S, _ = x.shape
    n_slab = D_ATTN // LANES
    row = lambda d: pl.BlockSpec((None, tile, d), lambda b, i: (b, i, 0))
    full = lambda shape: pl.BlockSpec(shape, lambda b, i: (0,) * len(shape))
    slab = pl.BlockSpec((None, n_slab, tile, LANES), lambda b, i: (b, 0, i, 0))
    return pl.pallas_call(
        _mix_in_kernel,
        grid=(B, S // tile),
        in_specs=[row(D_MODEL), full((1, D_MODEL)), full((D_MODEL, D_IN)), row(LANES), row(LANES)],
        out_specs=[slab, slab, slab, row(D_CONV)],
        out_shape=[jax.ShapeDtypeStruct((B, n_slab, S, LANES), F32)] * 3
        + [jax.ShapeDtypeStruct((B, S, D_CONV), F32)],
        compiler_params=_params(("parallel", "parallel")),
        name="mix_in",
    )(x, g, w_in, cos_t, sin_t)


def _attn_kernel(*refs, span, dil, first):
    if first:
        q_ref, kc_ref, vc_ref, kp_ref, vp_ref, o_ref, lse_ref = refs
    else:
        q_ref, kc_ref, vc_ref, kp_ref, vp_ref, oin_ref, lin_ref, o_ref, lse_ref = refs
    n = pl.program_id(2)
    blk = ATTN_BLOCK

    def rows(start, count):
        return pl.ds(start, count, stride=dil) if dil > 1 else pl.ds(start, count)

    lane = lax.broadcasted_iota(jnp.int32, (1, LANES), 1)
    head0 = lane < HEAD_DIM
    scale = HEAD_DIM ** -0.5
    qscale = [jnp.where(head0, scale, 0.0).astype(BF16), jnp.where(head0, 0.0, scale).astype(BF16)]
    qi = lax.broadcasted_iota(jnp.int32, (blk, 2 * blk), 0)
    kj = lax.broadcasted_iota(jnp.int32, (blk, 2 * blk), 1)
    dist = blk + qi - kj
    band = (dist >= 0) & (dist <= span)
    band_first = band & ((kj >= blk) | (n > 0))

    for r in range(dil):
        for qb in range(q_ref.shape[0] // (blk * dil)):
            mask = band_first if qb == 0 else band
            qrows = rows(qb * blk * dil + r, blk)
            q2 = q_ref[qrows, :].astype(BF16)
            if qb == 0:
                k2 = jnp.concatenate([kp_ref[rows(r, blk), :], kc_ref[rows(r, blk), :]], axis=0)
                v2 = jnp.concatenate([vp_ref[rows(r, blk), :], vc_ref[rows(r, blk), :]], axis=0)
            else:
                krows = rows((qb - 1) * blk * dil + r, 2 * blk)
                k2, v2 = kc_ref[krows, :], vc_ref[krows, :]
            k2, v2 = k2.astype(BF16), v2.astype(BF16)
            o_h, lse_h = [], []
            for hh in range(HEADS_PER_LANE_TILE):
                s = lax.dot_general(q2 * qscale[hh], k2, (((1,), (1,)), ((), ())),
                                    preferred_element_type=F32)
                s = jnp.where(mask, s, NEG_INF)
                m = jnp.max(s, axis=-1, keepdims=True)
                e = jnp.exp(s - m)
                l = jnp.sum(e, axis=-1, keepdims=True)
                o_h.append(jnp.dot(e.astype(BF16), v2, preferred_element_type=F32) / l)
                lse_h.append(m + jnp.log(l))
            o_new = jnp.where(head0, o_h[0], o_h[1])
            lse_new = jnp.where(head0, lse_h[0], lse_h[1])
            if first:
                o_ref[qrows, :] = o_new
                lse_ref[qrows, :] = lse_new
            else:
                lse_old = lin_ref[qrows, :]
                mx = jnp.maximum(lse_old, lse_new)
                w_old = jnp.exp(lse_old - mx)
                w_new = jnp.exp(lse_new - mx)
                tot = w_old + w_new
                o_ref[qrows, :] = (oin_ref[qrows, :] * w_old + o_new * w_new) / tot
                lse_ref[qrows, :] = mx + jnp.log(tot)


def _attn_pattern(q, k, v, acc, window, dilation, super_block):
    B, n_slab, S, _ = q.shape
    reach = ATTN_BLOCK * dilation
    assert super_block % reach == 0 and S % super_block == 0
    cur = pl.BlockSpec((None, None, super_block, LANES), lambda b, hp, n: (b, hp, n, 0))
    prev = pl.BlockSpec((None, None, reach, LANES),
                        lambda b, hp, n: (b, hp, jnp.maximum(n * (super_block // reach) - 1, 0), 0))
    first = acc is None
    ins = [q, k, v, k, v]
    in_specs = [cur, cur, cur, prev, prev]
    aliases = {}
    if not first:
        ins += list(acc)
        in_specs += [cur, cur]
        aliases = {5: 0, 6: 1}
    return pl.pallas_call(
        functools.partial(_attn_kernel, span=window // dilation, dil=dilation, first=first),
        grid=(B, n_slab, S // super_block),
        in_specs=in_specs,
        out_specs=[cur, cur],
        out_shape=[jax.ShapeDtypeStruct(q.shape, F32)] * 2,
        input_output_aliases=aliases,
        compiler_params=_params(("parallel", "parallel", "parallel")),
        name=f"attn_d{dilation}",
    )(*ins)


def _mix_out_kernel(x_ref, attn_ref, glu_ref, halo_ref, cw_ref, cb_ref, lng_ref, lnb_ref,
                    ga_ref, gc_ref, wo_ref, gffn_ref, h_ref, xnt_ref, win_ref, *, tile):
    i = pl.program_id(1)
    halo = halo_ref[...]
    win_ref[0:CONV_HALO, :] = jnp.where(i > 0, halo, jnp.zeros_like(halo))
    win_ref[CONV_HALO:, :] = glu_ref[...]
    off = CONV_HALO - (CONV_WIDTH - 1)
    c = jnp.zeros((tile, D_CONV), F32)
    for j in range(CONV_WIDTH):
        c = c + win_ref[pl.ds(off + j, tile), :] * cw_ref[j:j + 1, :]
    c = c + cb_ref[...]
    mu = jnp.mean(c, axis=-1, keepdims=True)
    var = jnp.mean(jnp.square(c - mu), axis=-1, keepdims=True)
    c = (c - mu) * lax.rsqrt(var + EPS) * lng_ref[...] + lnb_ref[...]
    c = c * jax.nn.sigmoid(c)
    attn = jnp.concatenate([attn_ref[s] for s in range(D_ATTN // LANES)], axis=1)
    y = jnp.dot(_rms(attn, ga_ref[...]).astype(BF16), wo_ref[0:D_ATTN, :],
                preferred_element_type=F32)
    y = y + jnp.dot(_rms(c, gc_ref[...]).astype(BF16), wo_ref[D_ATTN:, :],
                    preferred_element_type=F32)
    h = x_ref[...] + y
    h_ref[...] = h
    xnt_ref[...] = _rms(h, gffn_ref[...]).T.astype(BF16)


def _mix_out(x, attn, glu, conv_w, conv_b, ln_g, ln_b, g_attn, g_conv, w_out, g_ffn, tile):
    B, S, _ = x.shape
    nt = S // tile
    hpt = tile // CONV_HALO
    row = lambda d: pl.BlockSpec((None, tile, d), lambda b, i: (b, i, 0))
    full = lambda shape: pl.BlockSpec(shape, lambda b, i: (0,) * len(shape))
    halo = pl.BlockSpec((None, CONV_HALO, D_CONV), lambda b, i: (b, jnp.maximum(i * hpt - 1, 0), 0))
    return pl.pallas_call(
        functools.partial(_mix_out_kernel, tile=tile),
        grid=(B, nt),
        in_specs=[row(D_MODEL),
                  pl.BlockSpec((None, D_ATTN // LANES, tile, LANES), lambda b, i: (b, 0, i, 0)),
                  row(D_CONV), halo,
                  full((CONV_WIDTH, D_CONV)), full((1, D_CONV)), full((1, D_CONV)), full((1, D_CONV)),
                  full((1, D_ATTN)), full((1, D_CONV)), full((D_MODEL, D_MODEL)), full((1, D_MODEL))],
        out_specs=[row(D_MODEL), pl.BlockSpec((D_MODEL, tile), lambda b, i: (0, b * nt + i))],
        out_shape=[jax.ShapeDtypeStruct((B, S, D_MODEL), F32),
                   jax.ShapeDtypeStruct((D_MODEL, B * S), BF16)],
        scratch_shapes=[pltpu.VMEM((CONV_HALO + tile, D_CONV), F32)],
        compiler_params=_params(("parallel", "parallel")),
        name="mix_out",
    )(x, attn, glu, glu, conv_w, conv_b, ln_g, ln_b, g_attn, g_conv, w_out, g_ffn)


def _sorting_network(n):
    pairs, p = [], 1
    while p < n:
        k = p
        while k >= 1:
            for j in range(k % p, n - k, 2 * k):
                for i in range(min(k, n - j - k)):
                    if (i + j) // (2 * p) == (i + j + k) // (2 * p):
                        pairs.append((i + j, i + j + k))
            k //= 2
        p *= 2
    return pairs


def _sort_desc(v, pairs):
    v = list(v)
    for i, j in pairs:
        v[i], v[j] = jnp.maximum(v[i], v[j]), jnp.minimum(v[i], v[j])
    return v


def _bitonic_sort_desc(v):
    v = list(v)
    d = len(v) // 2
    while d >= 1:
        for i in range(len(v)):
            if i & d == 0:
                v[i], v[i + d] = jnp.maximum(v[i], v[i + d]), jnp.minimum(v[i], v[i + d])
        d //= 2
    return v


def _merge_sublanes(v, shift):
    n = len(v)
    return _bitonic_sort_desc([jnp.maximum(v[k], pltpu.roll(v[n - 1 - k], shift, 0)) for k in range(n)])


def _top16_over_rows(tiles):
    v = _sort_desc(tiles, _sorting_network(len(tiles)))
    for shift in (4, 2, 1):
        v = _merge_sublanes(v, shift)
    return v


def _peer_select_kernel(xt_ref, wq_ref, keys_ref, w1_ref, c_ref, w2_ref, r2_ref, q_ref):
    lt = pl.program_id(1)
    n_lt = q_ref.shape[0]

    @pl.when(lt == 0)
    def _():
        q = jnp.dot(wq_ref[...], xt_ref[...], preferred_element_type=F32).astype(BF16)
        for k in range(n_lt):
            q_ref[k] = q[:, k * LANES:(k + 1) * LANES]

    sub = lax.broadcasted_iota(jnp.int32, (SUB, LANES), 0)
    groups = N_KEYS // SUB

    def compact(v, lo):
        out = v[lo + SUB - 1]
        for s in range(SUB - 2, -1, -1):
            out = jnp.where(sub == s, v[lo + s], out)
        return out

    def head(h, carry):
        base = pl.multiple_of(h * 2 * N_KEYS, 2 * N_KEYS)
        s1 = jnp.dot(keys_ref[0], q_ref[lt, pl.ds(base, N_KEYS), :], preferred_element_type=F32)
        s2 = jnp.dot(keys_ref[1], q_ref[lt, pl.ds(base + N_KEYS, N_KEYS), :], preferred_element_type=F32)
        s1 = [s1[g * SUB:(g + 1) * SUB, :] for g in range(groups)]
        s2 = [s2[g * SUB:(g + 1) * SUB, :] for g in range(groups)]
        a = _top16_over_rows(s1)
        b = _top16_over_rows(s2)
        a_lo, a_hi, b_lo, b_hi = compact(a, 0), compact(a, SUB), compact(b, 0), compact(b, SUB)
        neg = jnp.full((SUB, LANES), NEG_INF, F32)
        c_j2 = jnp.where(sub < 5, a[2] + b_lo, neg)
        c_j3 = jnp.where(sub < 4, a[3] + b_lo, neg)
        c_j4 = jnp.where(sub < 3, a[4] + b_lo, neg)
        c_k0 = jnp.where(sub >= 5, a_lo + b[0], neg)
        c_k1 = jnp.where(sub >= 5, a_lo + b[1], neg)
        cands = [a[0] + b_lo, a[0] + b_hi, a[1] + b_lo,
                 jnp.where(sub < 5, c_j2, pltpu.roll(c_j4, 5, 0)),
                 jnp.where(sub < 4, c_j3, c_k0), c_k1, a_hi + b[0], neg]
        v = _sort_desc(cands, _sorting_network(len(cands)))
        v = _bitonic_sort_desc(v + [pltpu.roll(t, 4, 0) for t in reversed(v)])
        for shift in (2, 1):
            v = _merge_sublanes(v, shift)
        tau = v[PEER_TOPK - 1]
        top = a[0] + b[0]
        zsum = None
        for c in cands[:-1]:
            term = jnp.where(c >= tau, jnp.exp(c - top), 0.0)
            zsum = term if zsum is None else zsum + term
        for shift in (4, 2, 1):
            zsum = zsum + pltpu.roll(zsum, shift, 0)
        scale = 0.5 / zsum
        extra = []
        for j, kmax in ((0, 15), (1, 7), (2, 4)):
            e = jnp.zeros((SUB, LANES), F32)
            for k in range(4, kmax + 1):
                e = e + jnp.where(a[j] + b[k] >= tau, 1.0, 0.0)
            extra.append(e)
        cnt, rank, w1, w2 = [], [], [], []
        for g in range(groups):
            c = jnp.where(s1[g] == a[0], extra[0],
                          jnp.where(s1[g] == a[1], extra[1], jnp.where(s1[g] == a[2], extra[2], 0.0)))
            for k in range(4):
                c = c + jnp.where(s1[g] + b[k] >= tau, 1.0, 0.0)
            r = jnp.zeros((SUB, LANES), F32)
            for k in range(PEER_TOPK):
                r = r + jnp.where(b[k] > s2[g], 1.0, 0.0)
            cnt.append(c)
            rank.append(r)
            w1.append(jnp.exp(s1[g] - a[0]) * scale)
            w2.append(jnp.exp(s2[g] - b[0]))
        w1_ref[h] = jnp.concatenate(w1, axis=0)
        c_ref[h] = jnp.concatenate(cnt, axis=0)
        w2_ref[h] = jnp.concatenate(w2, axis=0).astype(BF16)
        r2_ref[h] = jnp.concatenate(rank, axis=0).astype(BF16)
        return carry

    lax.fori_loop(0, PEER_HEADS, head, 0)


def _peer_select(xnt, wq_t, keys, tile):
    n = xnt.shape[1]
    n_lt = tile // LANES
    tab = pl.BlockSpec((PEER_HEADS, N_KEYS, LANES), lambda i, lt: (0, 0, i * n_lt + lt))
    return pl.pallas_call(
        _peer_select_kernel,
        grid=(n // tile, n_lt),
        in_specs=[pl.BlockSpec((D_MODEL, tile), lambda i, lt: (0, i)),
                  pl.BlockSpec(wq_t.shape, lambda i, lt: (0, 0)),
                  pl.BlockSpec(keys.shape, lambda i, lt: (0, 0, 0))],
        out_specs=[tab, tab, tab, tab],
        out_shape=[jax.ShapeDtypeStruct((PEER_HEADS, N_KEYS, n), F32)] * 2
        + [jax.ShapeDtypeStruct((PEER_HEADS, N_KEYS, n), BF16)] * 2,
        scratch_shapes=[pltpu.VMEM((n_lt, PEER_HEADS * 2 * N_KEYS, LANES), BF16)],
        compiler_params=_params(("parallel", "arbitrary")),
        name="peer_select",
    )(xnt, wq_t, keys)


def _peer_dense_kernel(xt_ref, u_ref, vt_ref, w1_ref, c_ref, w2_ref, r2_ref, h_ref, o_ref,
                       hid0, hid1, act_ref, acc_ref, *, n_chunks):
    s = pl.program_id(0)
    tile = xt_ref.shape[1]
    rows_per_chunk = u_ref.shape[0] // N_KEYS

    @pl.when(s == 0)
    def _():
        for ref in (hid0, hid1, acc_ref):
            ref[...] = jnp.zeros_like(ref)

    def step(hid_next, hid_prev):
        hid_next[...] = jnp.dot(u_ref[...], xt_ref[...], preferred_element_type=F32)
        for row in range(rows_per_chunk):
            for lp in range(tile // PAIR):
                lanes = slice(lp * PAIR, (lp + 1) * PAIR)

                def bcast(ref, h):
                    parts = [jnp.broadcast_to(
                        ref[h, row:row + 1, lp * PAIR + k * LANES:lp * PAIR + (k + 1) * LANES],
                        (SUB, LANES)) for k in range(PAIR // LANES)]
                    return jnp.concatenate(parts, axis=1).astype(BF16)

                gate = [jnp.zeros((SUB, PAIR), BF16) for _ in range(N_KEYS // SUB)]
                for h in range(PEER_HEADS):
                    cnt = bcast(c_ref, h)
                    w1 = bcast(w1_ref, h)
                    for v in range(N_KEYS // SUB):
                        keys = slice(v * SUB, (v + 1) * SUB)
                        sel = jnp.where(r2_ref[h, keys, lanes] < cnt, w2_ref[h, keys, lanes],
                                        jnp.zeros((SUB, PAIR), BF16))
                        gate[v] = gate[v] + sel * w1
                for v in range(N_KEYS // SUB):
                    rows = slice(row * N_KEYS + v * SUB, row * N_KEYS + (v + 1) * SUB)
                    hid = hid_prev[rows, lanes]
                    gelu2 = hid + hid * lax.erf(hid * math.sqrt(0.5))
                    act_ref[rows, lanes] = gelu2.astype(BF16) * gate[v]
        acc_ref[...] += jnp.dot(vt_ref[...], act_ref[...], preferred_element_type=F32)

    @pl.when(s % 2 == 0)
    def _():
        step(hid0, hid1)

    @pl.when(s % 2 == 1)
    def _():
        step(hid1, hid0)

    @pl.when((s >= 1) & ((s - 1) % n_chunks == n_chunks - 1))
    def _():
        o_ref[...] = h_ref[...] + acc_ref[...].T
        acc_ref[...] = jnp.zeros_like(acc_ref)


def _peer_dense(xnt, u, vt, w1, cnt, w2, r2, h, tile, chunk):
    n = xnt.shape[1]
    n_chunks = N_EXPERTS // chunk
    steps = (n // tile) * n_chunks
    rows_per_chunk = chunk // N_KEYS
    at = lambda s, lag: jnp.clip(s - lag, 0, steps - 1)
    small = pl.BlockSpec((PEER_HEADS, rows_per_chunk, tile),
                         lambda s: (0, at(s, 1) % n_chunks, at(s, 1) // n_chunks))
    big = pl.BlockSpec((PEER_HEADS, N_KEYS, tile), lambda s: (0, 0, at(s, 1) // n_chunks))
    res = pl.BlockSpec((tile, D_MODEL), lambda s: (at(s, 1) // n_chunks, 0))
    return pl.pallas_call(
        functools.partial(_peer_dense_kernel, n_chunks=n_chunks),
        grid=(steps + 1,),
        in_specs=[pl.BlockSpec((D_MODEL, tile), lambda s: (0, at(s, 0) // n_chunks)),
                  pl.BlockSpec((chunk, D_MODEL), lambda s: (at(s, 0) % n_chunks, 0)),
                  pl.BlockSpec((D_MODEL, chunk), lambda s: (0, at(s, 1) % n_chunks)),
                  small, small, big, big, res],
        out_specs=res,
        out_shape=jax.ShapeDtypeStruct((n, D_MODEL), F32),
        scratch_shapes=[pltpu.VMEM((chunk, tile), F32), pltpu.VMEM((chunk, tile), F32),
                        pltpu.VMEM((chunk, tile), BF16), pltpu.VMEM((D_MODEL, tile), F32)],
        compiler_params=_params(("arbitrary",)),
        name="peer_dense",
    )(xnt, u, vt, w1, cnt, w2, r2, h)


def _ple_final_kernel(h_ref, p_ref, gple_ref, wg_ref, wp_ref, gfin_ref, o_ref, *, last):
    h = h_ref[...]
    gate = jax.nn.sigmoid(jnp.dot(_rms(h, gple_ref[...]).astype(BF16), wg_ref[...],
                                  preferred_element_type=F32))
    proj = jnp.dot(p_ref[...].astype(BF16), wp_ref[...], preferred_element_type=F32)
    h = h + gate * proj
    o_ref[...] = _rms(h, gfin_ref[...]) if last else h


def _ple_final(h, p, g_ple, w_gate, w_proj, g_final, tile, last):
    n = h.shape[0]
    full = lambda shape: pl.BlockSpec(shape, lambda i: (0,) * len(shape))
    return pl.pallas_call(
        functools.partial(_ple_final_kernel, last=last),
        grid=(n // tile,),
        in_specs=[pl.BlockSpec((tile, D_MODEL), lambda i: (i, 0)),
                  pl.BlockSpec((tile, PLE_DIM), lambda i: (i, 0)),
                  full((1, D_MODEL)), full((D_MODEL, D_MODEL)), full((PLE_DIM, D_MODEL)),
                  full((1, D_MODEL))],
        out_specs=pl.BlockSpec((tile, D_MODEL), lambda i: (i, 0)),
        out_shape=jax.ShapeDtypeStruct((n, D_MODEL), F32),
        compiler_params=_params(("parallel",)),
        name="ple_final",
    )(h, p, g_ple, w_gate, w_proj, g_final)


def _rope_tables(positions):
    half = ROPE_DIM // 2
    dim = jnp.arange(LANES) % HEAD_DIM
    inv_freq = jnp.where(dim < ROPE_DIM, ROPE_THETA ** (-2.0 * (dim % half).astype(F32) / ROPE_DIM), 0.0)
    ang = positions.astype(F32)[..., None] * inv_freq
    return jnp.cos(ang), jnp.sin(ang)


def kernel(x, p, positions, norm_mix, w_in, conv_w, conv_b, conv_ln_g, conv_ln_b, g_attn_out,
           g_conv_out, w_out, norm_ffn, peer_wq, sub_keys, expert_u, expert_v, norm_ple,
           w_ple_gate, w_ple_proj, final_norm):
    B, S, _ = x.shape
    depth = w_in.shape[0]
    n = B * S
    cos_t, sin_t = _rope_tables(positions)
    super_block = ATTN_BLOCK * max(d for _, d in DILATED_PATTERNS)
    vec = lambda t: t.reshape(1, -1)
    h = x
    for i in range(depth):
        q, k, v, glu = _mix_in(h, vec(norm_mix[i]), w_in[i].astype(BF16), cos_t, sin_t, tile=512)
        acc = None
        for window, dilation in DILATED_PATTERNS:
            acc = _attn_pattern(q, k, v, acc, window, dilation, super_block)
        h, xnt = _mix_out(h, acc[0], glu, conv_w[i], vec(conv_b[i]), vec(conv_ln_g[i]),
                          vec(conv_ln_b[i]), vec(g_attn_out[i]), vec(g_conv_out[i]),
                          w_out[i].astype(BF16), vec(norm_ffn[i]), tile=256)
        w1, cnt, w2, r2 = _peer_select(xnt, peer_wq[i].T.astype(BF16), sub_keys[i].astype(BF16),
                                       tile=256)
        h = _peer_dense(xnt, expert_u[i].astype(BF16), expert_v[i].T.astype(BF16), w1, cnt, w2, r2,
                        h.reshape(n, D_MODEL), tile=512, chunk=1024)
        h = _ple_final(h, p[i].reshape(n, PLE_DIM), vec(norm_ple[i]), w_ple_gate[i].astype(BF16),
                       w_ple_proj[i].astype(BF16), vec(final_norm), tile=512, last=(i == depth - 1))
        h = h.reshape(B, S, D_MODEL)
    return h
```

```python
import functools
import math

import jax
import jax.numpy as jnp
from jax import lax
from jax.experimental import pallas as pl
from jax.experimental.pallas import tpu as pltpu

F32 = jnp.float32
BF16 = jnp.bfloat16

D_MODEL = 1024
PLE_DIM = 256
HEAD_DIM = 64
N_ATTN_HEADS = 8
D_ATTN = N_ATTN_HEADS * HEAD_DIM
D_CONV = D_MODEL // 2
D_IN = 3 * D_ATTN + 2 * D_CONV
CONV_WIDTH = 31
ROPE_DIM = HEAD_DIM // 4
ROPE_THETA = 500000.0
DILATED_PATTERNS = ((128, 1), (512, 4), (2048, 16))
ATTN_BLOCK = 128
N_KEYS = 128
N_EXPERTS = N_KEYS * N_KEYS
PEER_HEADS = 8
PEER_TOPK = 16
EPS = 1e-6

LANES = 128
SUB = 8
PAIR = 2 * LANES
HEADS_PER_LANE_TILE = LANES // HEAD_DIM
CONV_HALO = 32
VMEM_LIMIT = 56 * 1024 * 1024

NEG_INF = float("-inf")


def _params(sem):
    return pltpu.CompilerParams(dimension_semantics=sem, vmem_limit_bytes=VMEM_LIMIT)


def _rms(x, g):
    return x * lax.rsqrt(jnp.mean(x * x, axis=-1, keepdims=True) + EPS) * g


def _mix_in_kernel(x_ref, g_ref, w_ref, cos_ref, sin_ref, q_ref, k_ref, v_ref, glu_ref):
    xn = _rms(x_ref[...], g_ref[...])
    z = jnp.dot(xn.astype(BF16), w_ref[...], preferred_element_type=F32)
    half = ROPE_DIM // 2
    cos, sin = cos_ref[...], sin_ref[...]
    dim = lax.broadcasted_iota(jnp.int32, (1, LANES), 1) % HEAD_DIM
    sa = jnp.where(dim < half, -sin, 0.0)
    sb = jnp.where((dim >= half) & (dim < ROPE_DIM), sin, 0.0)

    def rope(tt):
        return tt * cos + pltpu.roll(tt, LANES - half, 1) * sa + pltpu.roll(tt, half, 1) * sb

    for i in range(D_ATTN // LANES):
        q_ref[i] = rope(z[:, i * LANES:(i + 1) * LANES])
        k_ref[i] = rope(z[:, D_ATTN + i * LANES:D_ATTN + (i + 1) * LANES])
        v_ref[i] = z[:, 2 * D_ATTN + i * LANES:2 * D_ATTN + (i + 1) * LANES]
    a = z[:, 3 * D_ATTN:3 * D_ATTN + D_CONV]
    gt = z[:, 3 * D_ATTN + D_CONV:]
    glu_ref[...] = a * jax.nn.sigmoid(gt)


def _mix_in(x, g, w_in, cos_t, sin_t, tile):
    B, S, _ = x.shape
    n_slab = D_ATTN // LANES
    row = lambda d: pl.BlockSpec((None, tile, d), lambda b, i: (b, i, 0))
    full = lambda shape: pl.BlockSpec(shape, lambda b, i: (0,) * len(shape))
    slab = pl.BlockSpec((None, n_slab, tile, LANES), lambda b, i: (b, 0, i, 0))
    return pl.pallas_call(
        _mix_in_kernel,
        grid=(B, S // tile),
        in_specs=[row(D_MODEL), full((1, D_MODEL)), full((D_MODEL, D_IN)), row(LANES), row(LANES)],
        out_specs=[slab, slab, slab, row(D_CONV)],
        out_shape=[jax.ShapeDtypeStruct((B, n_slab, S, LANES), F32)] * 3
        + [jax.ShapeDtypeStruct((B, S, D_CONV), F32)],
        compiler_params=_params(("parallel", "parallel")),
        name="mix_in",
    )(x, g, w_in, cos_t, sin_t)


def _attn_kernel(*refs, span, dil, first):
    if first:
        q_ref, kc_ref, vc_ref, kp_ref, vp_ref, o_ref, lse_ref = refs
    else:
        q_ref, kc_ref, vc_ref, kp_ref, vp_ref, oin_ref, lin_ref, o_ref, lse_ref = refs
    n = pl.program_id(2)
    blk = ATTN_BLOCK

    def rows(start, count):
        return pl.ds(start, count, stride=dil) if dil > 1 else pl.ds(start, count)

    lane = lax.broadcasted_iota(jnp.int32, (1, LANES), 1)
    head0 = lane < HEAD_DIM
    scale = HEAD_DIM ** -0.5
    qscale = [jnp.where(head0, scale, 0.0).astype(BF16), jnp.where(head0, 0.0, scale).astype(BF16)]
    qi = lax.broadcasted_iota(jnp.int32, (blk, 2 * blk), 0)
    kj = lax.broadcasted_iota(jnp.int32, (blk, 2 * blk), 1)
    dist = blk + qi - kj
    band = (dist >= 0) & (dist <= span)
    band_first = band & ((kj >= blk) | (n > 0))

    for r in range(dil):
        for qb in range(q_ref.shape[0] // (blk * dil)):
            mask = band_first if qb == 0 else band
            qrows = rows(qb * blk * dil + r, blk)
            q2 = q_ref[qrows, :].astype(BF16)
            if qb == 0:
                k2 = jnp.concatenate([kp_ref[rows(r, blk), :], kc_ref[rows(r, blk), :]], axis=0)
                v2 = jnp.concatenate([vp_ref[rows(r, blk), :], vc_ref[rows(r, blk), :]], axis=0)
            else:
                krows = rows((qb - 1) * blk * dil + r, 2 * blk)
                k2, v2 = kc_ref[krows, :], vc_ref[krows, :]
            k2, v2 = k2.astype(BF16), v2.astype(BF16)
            o_h, lse_h = [], []
            for hh in range(HEADS_PER_LANE_TILE):
                s = lax.dot_general(q2 * qscale[hh], k2, (((1,), (1,)), ((), ())),
                                    preferred_element_type=F32)
                s = jnp.where(mask, s, NEG_INF)
                m = jnp.max(s, axis=-1, keepdims=True)
                e = jnp.exp(s - m)
                l = jnp.sum(e, axis=-1, keepdims=True)
                o_h.append(jnp.dot(e.astype(BF16), v2, preferred_element_type=F32) / l)
                lse_h.append(m + jnp.log(l))
            o_new = jnp.where(head0, o_h[0], o_h[1])
            lse_new = jnp.where(head0, lse_h[0], lse_h[1])
            if first:
                o_ref[qrows, :] = o_new
                lse_ref[qrows, :] = lse_new
            else:
                lse_old = lin_ref[qrows, :]
                mx = jnp.maximum(lse_old, lse_new)
                w_old = jnp.exp(lse_old - mx)
                w_new = jnp.exp(lse_new - mx)
                tot = w_old + w_new
                o_ref[qrows, :] = (oin_ref[qrows, :] * w_old + o_new * w_new) / tot
                lse_ref[qrows, :] = mx + jnp.log(tot)


def _attn_pattern(q, k, v, acc, window, dilation, super_block):
    B, n_slab, S, _ = q.shape
    reach = ATTN_BLOCK * dilation
    assert super_block % reach == 0 and S % super_block == 0
    cur = pl.BlockSpec((None, None, super_block, LANES), lambda b, hp, n: (b, hp, n, 0))
    prev = pl.BlockSpec((None, None, reach, LANES),
                        lambda b, hp, n: (b, hp, jnp.maximum(n * (super_block // reach) - 1, 0), 0))
    first = acc is None
    ins = [q, k, v, k, v]
    in_specs = [cur, cur, cur, prev, prev]
    aliases = {}
    if not first:
        ins += list(acc)
        in_specs += [cur, cur]
        aliases = {5: 0, 6: 1}
    return pl.pallas_call(
        functools.partial(_attn_kernel, span=window // dilation, dil=dilation, first=first),
        grid=(B, n_slab, S // super_block),
        in_specs=in_specs,
        out_specs=[cur, cur],
        out_shape=[jax.ShapeDtypeStruct(q.shape, F32)] * 2,
        input_output_aliases=aliases,
        compiler_params=_params(("parallel", "parallel", "parallel")),
        name=f"attn_d{dilation}",
    )(*ins)


def _mix_out_kernel(x_ref, attn_ref, glu_ref, halo_ref, cw_ref, cb_ref, lng_ref, lnb_ref,
                    ga_ref, gc_ref, wo_ref, gffn_ref, h_ref, xnt_ref, win_ref, rot_ref, *, tile):
    i = pl.program_id(1)
    halo = halo_ref[...]
    win_ref[0:CONV_HALO, :] = jnp.where(i > 0, halo, jnp.zeros_like(halo))
    win_ref[CONV_HALO:, :] = glu_ref[...]
    span = rot_ref.shape[1]
    for shift in range(1, SUB):
        rot_ref[shift - 1] = win_ref[pl.ds(shift, span), :]
    off = CONV_HALO - (CONV_WIDTH - 1)
    c = jnp.zeros((tile, D_CONV), F32)
    for j in range(CONV_WIDTH):
        shift, base = (off + j) % SUB, (off + j) // SUB * SUB
        rows = win_ref[base:base + tile, :] if shift == 0 else rot_ref[shift - 1, base:base + tile, :]
        c = c + rows * cw_ref[j:j + 1, :]
    c = c + cb_ref[...]
    mu = jnp.mean(c, axis=-1, keepdims=True)
    var = jnp.mean(jnp.square(c - mu), axis=-1, keepdims=True)
    c = (c - mu) * lax.rsqrt(var + EPS) * lng_ref[...] + lnb_ref[...]
    c = c * jax.nn.sigmoid(c)
    attn = jnp.concatenate([attn_ref[s] for s in range(D_ATTN // LANES)], axis=1)
    y = jnp.dot(_rms(attn, ga_ref[...]).astype(BF16), wo_ref[0:D_ATTN, :],
                preferred_element_type=F32)
    y = y + jnp.dot(_rms(c, gc_ref[...]).astype(BF16), wo_ref[D_ATTN:, :],
                    preferred_element_type=F32)
    h = x_ref[...] + y
    h_ref[...] = h
    xnt_ref[...] = _rms(h, gffn_ref[...]).T.astype(BF16)


def _mix_out(x, attn, glu, conv_w, conv_b, ln_g, ln_b, g_attn, g_conv, w_out, g_ffn, tile):
    B, S, _ = x.shape
    nt = S // tile
    hpt = tile // CONV_HALO
    row = lambda d: pl.BlockSpec((None, tile, d), lambda b, i: (b, i, 0))
    full = lambda shape: pl.BlockSpec(shape, lambda b, i: (0,) * len(shape))
    halo = pl.BlockSpec((None, CONV_HALO, D_CONV), lambda b, i: (b, jnp.maximum(i * hpt - 1, 0), 0))
    return pl.pallas_call(
        functools.partial(_mix_out_kernel, tile=tile),
        grid=(B, nt),
        in_specs=[row(D_MODEL),
                  pl.BlockSpec((None, D_ATTN // LANES, tile, LANES), lambda b, i: (b, 0, i, 0)),
                  row(D_CONV), halo,
                  full((CONV_WIDTH, D_CONV)), full((1, D_CONV)), full((1, D_CONV)), full((1, D_CONV)),
                  full((1, D_ATTN)), full((1, D_CONV)), full((D_MODEL, D_MODEL)), full((1, D_MODEL))],
        out_specs=[row(D_MODEL), pl.BlockSpec((D_MODEL, tile), lambda b, i: (0, b * nt + i))],
        out_shape=[jax.ShapeDtypeStruct((B, S, D_MODEL), F32),
                   jax.ShapeDtypeStruct((D_MODEL, B * S), BF16)],
        scratch_shapes=[pltpu.VMEM((CONV_HALO + tile, D_CONV), F32),
                        pltpu.VMEM((SUB - 1, CONV_HALO + tile - SUB, D_CONV), F32)],
        compiler_params=_params(("parallel", "parallel")),
        name="mix_out",
    )(x, attn, glu, glu, conv_w, conv_b, ln_g, ln_b, g_attn, g_conv, w_out, g_ffn)


def _sorting_network(n):
    pairs, p = [], 1
    while p < n:
        k = p
        while k >= 1:
            for j in range(k % p, n - k, 2 * k):
                for i in range(min(k, n - j - k)):
                    if (i + j) // (2 * p) == (i + j + k) // (2 * p):
                        pairs.append((i + j, i + j + k))
            k //= 2
        p *= 2
    return pairs


def _sort_desc(v, pairs):
    v = list(v)
    for i, j in pairs:
        v[i], v[j] = jnp.maximum(v[i], v[j]), jnp.minimum(v[i], v[j])
    return v


def _bitonic_sort_desc(v):
    v = list(v)
    d = len(v) // 2
    while d >= 1:
        for i in range(len(v)):
            if i & d == 0:
                v[i], v[i + d] = jnp.maximum(v[i], v[i + d]), jnp.minimum(v[i], v[i + d])
        d //= 2
    return v


def _merge_sublanes(v, shift):
    n = len(v)
    return _bitonic_sort_desc([jnp.maximum(v[k], pltpu.roll(v[n - 1 - k], shift, 0)) for k in range(n)])


def _top16_over_rows(tiles):
    v = _sort_desc(tiles, _sorting_network(len(tiles)))
    for shift in (4, 2, 1):
        v = _merge_sublanes(v, shift)
    return v


def _peer_select_kernel(xt_ref, wq_ref, keys_ref, w1_ref, c_ref, w2_ref, r2_ref, q_ref):
    lt = pl.program_id(1)
    n_lt = q_ref.shape[0]

    @pl.when(lt == 0)
    def _():
        q = jnp.dot(wq_ref[...], xt_ref[...], preferred_element_type=F32).astype(BF16)
        for k in range(n_lt):
            q_ref[k] = q[:, k * LANES:(k + 1) * LANES]

    sub = lax.broadcasted_iota(jnp.int32, (SUB, LANES), 0)
    groups = N_KEYS // SUB

    def compact(v, lo):
        out = v[lo + SUB - 1]
        for s in range(SUB - 2, -1, -1):
            out = jnp.where(sub == s, v[lo + s], out)
        return out

    def head(h, carry):
        base = pl.multiple_of(h * 2 * N_KEYS, 2 * N_KEYS)
        s1 = jnp.dot(keys_ref[0], q_ref[lt, pl.ds(base, N_KEYS), :], preferred_element_type=F32)
        s2 = jnp.dot(keys_ref[1], q_ref[lt, pl.ds(base + N_KEYS, N_KEYS), :], preferred_element_type=F32)
        s1 = [s1[g * SUB:(g + 1) * SUB, :] for g in range(groups)]
        s2 = [s2[g * SUB:(g + 1) * SUB, :] for g in range(groups)]
        a = _top16_over_rows(s1)
        b = _top16_over_rows(s2)
        a_lo, a_hi, b_lo, b_hi = compact(a, 0), compact(a, SUB), compact(b, 0), compact(b, SUB)
        neg = jnp.full((SUB, LANES), NEG_INF, F32)
        c_j2 = jnp.where(sub < 5, a[2] + b_lo, neg)
        c_j3 = jnp.where(sub < 4, a[3] + b_lo, neg)
        c_j4 = jnp.where(sub < 3, a[4] + b_lo, neg)
        c_k0 = jnp.where(sub >= 5, a_lo + b[0], neg)
        c_k1 = jnp.where(sub >= 5, a_lo + b[1], neg)
        cands = [a[0] + b_lo, a[0] + b_hi, a[1] + b_lo,
                 jnp.where(sub < 5, c_j2, pltpu.roll(c_j4, 5, 0)),
                 jnp.where(sub < 4, c_j3, c_k0), c_k1, a_hi + b[0], neg]
        v = _sort_desc(cands, _sorting_network(len(cands)))
        v = _bitonic_sort_desc(v + [pltpu.roll(t, 4, 0) for t in reversed(v)])
        for shift in (2, 1):
            v = _merge_sublanes(v, shift)
        tau = v[PEER_TOPK - 1]
        top = a[0] + b[0]
        zsum = None
        for c in cands[:-1]:
            term = jnp.where(c >= tau, jnp.exp(c - top), 0.0)
            zsum = term if zsum is None else zsum + term
        for shift in (4, 2, 1):
            zsum = zsum + pltpu.roll(zsum, shift, 0)
        scale = 0.5 / zsum
        extra = []
        for j, kmax in ((0, 15), (1, 7), (2, 4)):
            e = jnp.zeros((SUB, LANES), F32)
            for k in range(4, kmax + 1):
                e = e + jnp.where(a[j] + b[k] >= tau, 1.0, 0.0)
            extra.append(e)
        cnt, rank, w1, w2 = [], [], [], []
        for g in range(groups):
            c = jnp.where(s1[g] == a[0], extra[0],
                          jnp.where(s1[g] == a[1], extra[1], jnp.where(s1[g] == a[2], extra[2], 0.0)))
            for k in range(4):
                c = c + jnp.where(s1[g] + b[k] >= tau, 1.0, 0.0)
            r = jnp.zeros((SUB, LANES), F32)
            for k in range(PEER_TOPK):
                r = r + jnp.where(b[k] > s2[g], 1.0, 0.0)
            cnt.append(c)
            rank.append(r)
            w1.append(jnp.exp(s1[g] - a[0]) * scale)
            w2.append(jnp.exp(s2[g] - b[0]))
        w1_ref[h] = jnp.concatenate(w1, axis=0)
        c_ref[h] = jnp.concatenate(cnt, axis=0)
        w2_ref[h] = jnp.concatenate(w2, axis=0).astype(BF16)
        r2_ref[h] = jnp.concatenate(rank, axis=0).astype(BF16)
        return carry

    def head_pair(i, carry):
        head(2 * i, carry)
        return head(2 * i + 1, carry)

    lax.fori_loop(0, PEER_HEADS // 2, head_pair, 0)


def _peer_select(xnt, wq_t, keys, tile):
    n = xnt.shape[1]
    n_lt = tile // LANES
    tab = pl.BlockSpec((PEER_HEADS, N_KEYS, LANES), lambda i, lt: (0, 0, i * n_lt + lt))
    return pl.pallas_call(
        _peer_select_kernel,
        grid=(n // tile, n_lt),
        in_specs=[pl.BlockSpec((D_MODEL, tile), lambda i, lt: (0, i)),
                  pl.BlockSpec(wq_t.shape, lambda i, lt: (0, 0)),
                  pl.BlockSpec(keys.shape, lambda i, lt: (0, 0, 0))],
        out_specs=[tab, tab, tab, tab],
        out_shape=[jax.ShapeDtypeStruct((PEER_HEADS, N_KEYS, n), F32)] * 2
        + [jax.ShapeDtypeStruct((PEER_HEADS, N_KEYS, n), BF16)] * 2,
        scratch_shapes=[pltpu.VMEM((n_lt, PEER_HEADS * 2 * N_KEYS, LANES), BF16)],
        compiler_params=_params(("parallel", "arbitrary")),
        name="peer_select",
    )(xnt, wq_t, keys)


def _peer_dense_kernel(xt_ref, u_ref, vt_ref, w1_ref, c_ref, w2_ref, r2_ref, h_ref, o_ref,
                       hid_ref, act_ref, acc_ref):
    j = pl.program_id(1)
    tile = xt_ref.shape[1]
    rows_per_chunk = u_ref.shape[0] // N_KEYS

    @pl.when(j == 0)
    def _():
        acc_ref[...] = jnp.zeros_like(acc_ref)

    hid_ref[...] = jnp.dot(u_ref[...], xt_ref[...], preferred_element_type=F32)
    for row in range(rows_per_chunk):
        for lp in range(tile // PAIR):
            lanes = slice(lp * PAIR, (lp + 1) * PAIR)

            def bcast(ref, h):
                parts = [jnp.broadcast_to(ref[h, row:row + 1, lp * PAIR + k * LANES:lp * PAIR + (k + 1) * LANES],
                                          (SUB, LANES)) for k in range(PAIR // LANES)]
                return jnp.concatenate(parts, axis=1).astype(BF16)

            gate = [jnp.zeros((SUB, PAIR), BF16) for _ in range(N_KEYS // SUB)]
            for h in range(PEER_HEADS):
                cnt = bcast(c_ref, h)
                w1 = bcast(w1_ref, h)
                for v in range(N_KEYS // SUB):
                    keys = slice(v * SUB, (v + 1) * SUB)
                    sel = jnp.where(r2_ref[h, keys, lanes] < cnt, w2_ref[h, keys, lanes],
                                    jnp.zeros((SUB, PAIR), BF16))
                    gate[v] = gate[v] + sel * w1
            for v in range(N_KEYS // SUB):
                rows = slice(row * N_KEYS + v * SUB, row * N_KEYS + (v + 1) * SUB)
                hid = hid_ref[rows, lanes]
                gelu2 = hid + hid * lax.erf(hid * math.sqrt(0.5))
                act_ref[rows, lanes] = gelu2.astype(BF16) * gate[v]
    acc_ref[...] += jnp.dot(vt_ref[...], act_ref[...], preferred_element_type=F32)

    @pl.when(j == pl.num_programs(1) - 1)
    def _():
        o_ref[...] = h_ref[...] + acc_ref[...].T


def _peer_dense(xnt, u, vt, w1, cnt, w2, r2, h, tile, chunk):
    n = xnt.shape[1]
    rows_per_chunk = chunk // N_KEYS
    small = pl.BlockSpec((PEER_HEADS, rows_per_chunk, tile), lambda i, j: (0, j, i))
    big = pl.BlockSpec((PEER_HEADS, N_KEYS, tile), lambda i, j: (0, 0, i))
    res = pl.BlockSpec((tile, D_MODEL), lambda i, j: (i, 0))
    return pl.pallas_call(
        _peer_dense_kernel,
        grid=(n // tile, N_EXPERTS // chunk),
        in_specs=[pl.BlockSpec((D_MODEL, tile), lambda i, j: (0, i)),
                  pl.BlockSpec((chunk, D_MODEL), lambda i, j: (j, 0)),
                  pl.BlockSpec((D_MODEL, chunk), lambda i, j: (0, j)),
                  small, small, big, big, res],
        out_specs=res,
        out_shape=jax.ShapeDtypeStruct((n, D_MODEL), F32),
        scratch_shapes=[pltpu.VMEM((chunk, tile), F32), pltpu.VMEM((chunk, tile), BF16),
                        pltpu.VMEM((D_MODEL, tile), F32)],
        compiler_params=_params(("parallel", "arbitrary")),
        name="peer_dense",
    )(xnt, u, vt, w1, cnt, w2, r2, h)


def _ple_final_kernel(h_ref, p_ref, gple_ref, wg_ref, wp_ref, gfin_ref, o_ref, *, last):
    h = h_ref[...]
    gate = jax.nn.sigmoid(jnp.dot(_rms(h, gple_ref[...]).astype(BF16), wg_ref[...],
                                  preferred_element_type=F32))
    proj = jnp.dot(p_ref[...].astype(BF16), wp_ref[...], preferred_element_type=F32)
    h = h + gate * proj
    o_ref[...] = _rms(h, gfin_ref[...]) if last else h


def _ple_final(h, p, g_ple, w_gate, w_proj, g_final, tile, last):
    n = h.shape[0]
    full = lambda shape: pl.BlockSpec(shape, lambda i: (0,) * len(shape))
    return pl.pallas_call(
        functools.partial(_ple_final_kernel, last=last),
        grid=(n // tile,),
        in_specs=[pl.BlockSpec((tile, D_MODEL), lambda i: (i, 0)),
                  pl.BlockSpec((tile, PLE_DIM), lambda i: (i, 0)),
                  full((1, D_MODEL)), full((D_MODEL, D_MODEL)), full((PLE_DIM, D_MODEL)),
                  full((1, D_MODEL))],
        out_specs=pl.BlockSpec((tile, D_MODEL), lambda i: (i, 0)),
        out_shape=jax.ShapeDtypeStruct((n, D_MODEL), F32),
        compiler_params=_params(("parallel",)),
        name="ple_final",
    )(h, p, g_ple, w_gate, w_proj, g_final)


def _rope_tables(positions):
    half = ROPE_DIM // 2
    dim = jnp.arange(LANES) % HEAD_DIM
    inv_freq = jnp.where(dim < ROPE_DIM, ROPE_THETA ** (-2.0 * (dim % half).astype(F32) / ROPE_DIM), 0.0)
    ang = positions.astype(F32)[..., None] * inv_freq
    return jnp.cos(ang), jnp.sin(ang)


def kernel(x, p, positions, norm_mix, w_in, conv_w, conv_b, conv_ln_g, conv_ln_b, g_attn_out,
           g_conv_out, w_out, norm_ffn, peer_wq, sub_keys, expert_u, expert_v, norm_ple,
           w_ple_gate, w_ple_proj, final_norm):
    B, S, _ = x.shape
    depth = w_in.shape[0]
    n = B * S
    cos_t, sin_t = _rope_tables(positions)
    super_block = ATTN_BLOCK * max(d for _, d in DILATED_PATTERNS)
    vec = lambda t: t.reshape(1, -1)
    h = x
    for i in range(depth):
        q, k, v, glu = _mix_in(h, vec(norm_mix[i]), w_in[i].astype(BF16), cos_t, sin_t, tile=512)
        acc = None
        for window, dilation in DILATED_PATTERNS:
            acc = _attn_pattern(q, k, v, acc, window, dilation, super_block)
        h, xnt = _mix_out(h, acc[0], glu, conv_w[i], vec(conv_b[i]), vec(conv_ln_g[i]),
                          vec(conv_ln_b[i]), vec(g_attn_out[i]), vec(g_conv_out[i]),
                          w_out[i].astype(BF16), vec(norm_ffn[i]), tile=256)
        w1, cnt, w2, r2 = _peer_select(xnt, peer_wq[i].T.astype(BF16), sub_keys[i].astype(BF16),
                                       tile=256)
        h = _peer_dense(xnt, expert_u[i].astype(BF16), expert_v[i].T.astype(BF16), w1, cnt, w2, r2,
                        h.reshape(n, D_MODEL), tile=512, chunk=1024)
        h = _ple_final(h, p[i].reshape(n, PLE_DIM), vec(norm_ple[i]), w_ple_gate[i].astype(BF16),
                       w_ple_proj[i].astype(BF16), vec(final_norm), tile=512, last=(i == depth - 1))
        h = h.reshape(B, S, D_MODEL)
    return h
```

```python
import functools
import math

import jax
import jax.numpy as jnp
from jax import lax
from jax.experimental import pallas as pl
from jax.experimental.pallas import tpu as pltpu

F32 = jnp.float32
BF16 = jnp.bfloat16

D_MODEL = 1024
PLE_DIM = 256
HEAD_DIM = 64
N_ATTN_HEADS = 8
D_ATTN = N_ATTN_HEADS * HEAD_DIM
D_CONV = D_MODEL // 2
D_IN = 3 * D_ATTN + 2 * D_CONV
CONV_WIDTH = 31
ROPE_DIM = HEAD_DIM // 4
ROPE_THETA = 500000.0
DILATED_PATTERNS = ((128, 1), (512, 4), (2048, 16))
ATTN_BLOCK = 128
N_KEYS = 128
N_EXPERTS = N_KEYS * N_KEYS
PEER_HEADS = 8
PEER_TOPK = 16
EPS = 1e-6

LANES = 128
SUB = 8
PAIR = 2 * LANES
HEADS_PER_LANE_TILE = LANES // HEAD_DIM
CONV_HALO = 32
VMEM_LIMIT = 56 * 1024 * 1024

NEG_INF = float("-inf")


def _params(sem):
    return pltpu.CompilerParams(dimension_semantics=sem, vmem_limit_bytes=VMEM_LIMIT)


def _rms(x, g):
    return x * lax.rsqrt(jnp.mean(x * x, axis=-1, keepdims=True) + EPS) * g


def _mix_in_kernel(x_ref, g_ref, w_ref, cos_ref, sin_ref, q_ref, k_ref, v_ref, glu_ref):
    xn = _rms(x_ref[...], g_ref[...])
    z = jnp.dot(xn.astype(BF16), w_ref[...], preferred_element_type=F32)
    half = ROPE_DIM // 2
    cos, sin = cos_ref[...], sin_ref[...]
    dim = lax.broadcasted_iota(jnp.int32, (1, LANES), 1) % HEAD_DIM
    sa = jnp.where(dim < half, -sin, 0.0)
    sb = jnp.where((dim >= half) & (dim < ROPE_DIM), sin, 0.0)

    def rope(tt):
        return tt * cos + pltpu.roll(tt, LANES - half, 1) * sa + pltpu.roll(tt, half, 1) * sb

    for i in range(D_ATTN // LANES):
        q_ref[i] = rope(z[:, i * LANES:(i + 1) * LANES])
        k_ref[i] = rope(z[:, D_ATTN + i * LANES:D_ATTN + (i + 1) * LANES])
        v_ref[i] = z[:, 2 * D_ATTN + i * LANES:2 * D_ATTN + (i + 1) * LANES]
    a = z[:, 3 * D_ATTN:3 * D_ATTN + D_CONV]
    gt = z[:, 3 * D_ATTN + D_CONV:]
    glu_ref[...] = a * jax.nn.sigmoid(gt)


def _mix_in(x, g, w_in, cos_t, sin_t, tile):
    B, S, _ = x.shape
    n_slab = D_ATTN // LANES
    row = lambda d: pl.BlockSpec((None, tile, d), lambda b, i: (b, i, 0))
    full = lambda shape: pl.BlockSpec(shape, lambda b, i: (0,) * len(shape))
    slab = pl.BlockSpec((None, n_slab, tile, LANES), lambda b, i: (b, 0, i, 0))
    return pl.pallas_call(
        _mix_in_kernel,
        grid=(B, S // tile),
        in_specs=[row(D_MODEL), full((1, D_MODEL)), full((D_MODEL, D_IN)), row(LANES), row(LANES)],
        out_specs=[slab, slab, slab, row(D_CONV)],
        out_shape=[jax.ShapeDtypeStruct((B, n_slab, S, LANES), F32)] * 3
        + [jax.ShapeDtypeStruct((B, S, D_CONV), F32)],
        compiler_params=_params(("parallel", "parallel")),
        name="mix_in",
    )(x, g, w_in, cos_t, sin_t)


COARSE_STRIDE = 4


def _attn_kernel(*refs, span, dil, first, coarse):
    n_in = 5 if first else 7
    ins, outs, staged = list(refs[:n_in]), list(refs[n_in:n_in + 2]), list(refs[n_in + 2:])
    n = pl.program_id(2)
    blk = ATTN_BLOCK
    fine = dil // coarse
    if coarse > 1:
        for src, dst in zip(ins, staged[:n_in]):
            for p in range(coarse):
                dst[p] = src[pl.ds(p, src.shape[0] // coarse, stride=coarse), :]
        ins, out_stage = staged[:n_in], staged[n_in:]
    else:
        out_stage = outs

    def view(ref, r, first_elem, count):
        if coarse > 1:
            return ref, (r % coarse, pl.ds(r // coarse + fine * first_elem, count, stride=fine), slice(None))
        if dil > 1:
            return ref, (pl.ds(r + dil * first_elem, count, stride=dil), slice(None))
        return ref, (pl.ds(first_elem, count), slice(None))

    def take(ref, r, first_elem, count):
        ref, idx = view(ref, r, first_elem, count)
        return ref[idx]

    def put(ref, r, first_elem, value):
        ref, idx = view(ref, r, first_elem, value.shape[0])
        ref[idx] = value

    q_ref, kc_ref, vc_ref, kp_ref, vp_ref = ins[:5]
    lane = lax.broadcasted_iota(jnp.int32, (1, LANES), 1)
    head0 = lane < HEAD_DIM
    scale = HEAD_DIM ** -0.5
    qscale = [jnp.where(head0, scale, 0.0).astype(BF16), jnp.where(head0, 0.0, scale).astype(BF16)]
    qi = lax.broadcasted_iota(jnp.int32, (blk, 2 * blk), 0)
    kj = lax.broadcasted_iota(jnp.int32, (blk, 2 * blk), 1)
    dist = blk + qi - kj
    band = (dist >= 0) & (dist <= span)
    band_first = band & ((kj >= blk) | (n > 0))

    for r in range(dil):
        for qb in range(outs[0].shape[0] // (blk * dil)):
            mask = band_first if qb == 0 else band
            q2 = take(q_ref, r, qb * blk, blk).astype(BF16)
            if qb == 0:
                k2 = jnp.concatenate([take(kp_ref, r, 0, blk), take(kc_ref, r, 0, blk)], axis=0)
                v2 = jnp.concatenate([take(vp_ref, r, 0, blk), take(vc_ref, r, 0, blk)], axis=0)
            else:
                k2 = take(kc_ref, r, (qb - 1) * blk, 2 * blk)
                v2 = take(vc_ref, r, (qb - 1) * blk, 2 * blk)
            k2, v2 = k2.astype(BF16), v2.astype(BF16)
            o_h, lse_h = [], []
            for hh in range(HEADS_PER_LANE_TILE):
                s = lax.dot_general(q2 * qscale[hh], k2, (((1,), (1,)), ((), ())),
                                    preferred_element_type=F32)
                s = jnp.where(mask, s, NEG_INF)
                m = jnp.max(s, axis=-1, keepdims=True)
                e = jnp.exp(s - m)
                l = jnp.sum(e, axis=-1, keepdims=True)
                o_h.append(jnp.dot(e.astype(BF16), v2, preferred_element_type=F32) / l)
                lse_h.append(m + jnp.log(l))
            o_new = jnp.where(head0, o_h[0], o_h[1])
            lse_new = jnp.where(head0, lse_h[0], lse_h[1])
            if not first:
                lse_old = take(ins[6], r, qb * blk, blk)
                mx = jnp.maximum(lse_old, lse_new)
                w_old = jnp.exp(lse_old - mx)
                w_new = jnp.exp(lse_new - mx)
                tot = w_old + w_new
                o_new = (take(ins[5], r, qb * blk, blk) * w_old + o_new * w_new) / tot
                lse_new = mx + jnp.log(tot)
            put(out_stage[0], r, qb * blk, o_new)
            put(out_stage[1], r, qb * blk, lse_new)

    if coarse > 1:
        for src, dst in zip(out_stage, outs):
            for p in range(coarse):
                dst[pl.ds(p, dst.shape[0] // coarse, stride=coarse), :] = src[p]


def _attn_pattern(q, k, v, acc, window, dilation, super_block):
    B, n_slab, S, _ = q.shape
    reach = ATTN_BLOCK * dilation
    assert super_block % reach == 0 and S % super_block == 0
    cur = pl.BlockSpec((None, None, super_block, LANES), lambda b, hp, n: (b, hp, n, 0))
    prev = pl.BlockSpec((None, None, reach, LANES),
                        lambda b, hp, n: (b, hp, jnp.maximum(n * (super_block // reach) - 1, 0), 0))
    first = acc is None
    ins = [q, k, v, k, v]
    in_specs = [cur, cur, cur, prev, prev]
    aliases = {}
    if not first:
        ins += list(acc)
        in_specs += [cur, cur]
        aliases = {5: 0, 6: 1}
    coarse = COARSE_STRIDE if dilation > COARSE_STRIDE else 1
    scratch = []
    if coarse > 1:
        assert dilation % coarse == 0
        block_rows = [spec.block_shape[2] for spec in in_specs] + [super_block] * 2
        scratch = [pltpu.VMEM((coarse, rows // coarse, LANES), F32) for rows in block_rows]
    return pl.pallas_call(
        functools.partial(_attn_kernel, span=window // dilation, dil=dilation, first=first, coarse=coarse),
        grid=(B, n_slab, S // super_block),
        in_specs=in_specs,
        out_specs=[cur, cur],
        out_shape=[jax.ShapeDtypeStruct(q.shape, F32)] * 2,
        scratch_shapes=scratch,
        input_output_aliases=aliases,
        compiler_params=_params(("parallel", "parallel", "parallel")),
        name=f"attn_d{dilation}",
    )(*ins)


def _mix_out_kernel(x_ref, attn_ref, glu_ref, halo_ref, cw_ref, cb_ref, lng_ref, lnb_ref,
                    ga_ref, gc_ref, wo_ref, gffn_ref, h_ref, xnt_ref, win_ref, rot_ref, *, tile):
    i = pl.program_id(1)
    halo = halo_ref[...]
    win_ref[0:CONV_HALO, :] = jnp.where(i > 0, halo, jnp.zeros_like(halo))
    win_ref[CONV_HALO:, :] = glu_ref[...]
    span = rot_ref.shape[1]
    for shift in range(1, SUB):
        rot_ref[shift - 1] = win_ref[pl.ds(shift, span), :]
    off = CONV_HALO - (CONV_WIDTH - 1)
    c = jnp.zeros((tile, D_CONV), F32)
    for j in range(CONV_WIDTH):
        shift, base = (off + j) % SUB, (off + j) // SUB * SUB
        rows = win_ref[base:base + tile, :] if shift == 0 else rot_ref[shift - 1, base:base + tile, :]
        c = c + rows * cw_ref[j:j + 1, :]
    c = c + cb_ref[...]
    mu = jnp.mean(c, axis=-1, keepdims=True)
    var = jnp.mean(jnp.square(c - mu), axis=-1, keepdims=True)
    c = (c - mu) * lax.rsqrt(var + EPS) * lng_ref[...] + lnb_ref[...]
    c = c * jax.nn.sigmoid(c)
    attn = jnp.concatenate([attn_ref[s] for s in range(D_ATTN // LANES)], axis=1)
    y = jnp.dot(_rms(attn, ga_ref[...]).astype(BF16), wo_ref[0:D_ATTN, :],
                preferred_element_type=F32)
    y = y + jnp.dot(_rms(c, gc_ref[...]).astype(BF16), wo_ref[D_ATTN:, :],
                    preferred_element_type=F32)
    h = x_ref[...] + y
    h_ref[...] = h
    xnt_ref[...] = _rms(h, gffn_ref[...]).T.astype(BF16)


def _mix_out(x, attn, glu, conv_w, conv_b, ln_g, ln_b, g_attn, g_conv, w_out, g_ffn, tile):
    B, S, _ = x.shape
    nt = S // tile
    hpt = tile // CONV_HALO
    row = lambda d: pl.BlockSpec((None, tile, d), lambda b, i: (b, i, 0))
    full = lambda shape: pl.BlockSpec(shape, lambda b, i: (0,) * len(shape))
    halo = pl.BlockSpec((None, CONV_HALO, D_CONV), lambda b, i: (b, jnp.maximum(i * hpt - 1, 0), 0))
    return pl.pallas_call(
        functools.partial(_mix_out_kernel, tile=tile),
        grid=(B, nt),
        in_specs=[row(D_MODEL),
                  pl.BlockSpec((None, D_ATTN // LANES, tile, LANES), lambda b, i: (b, 0, i, 0)),
                  row(D_CONV), halo,
                  full((CONV_WIDTH, D_CONV)), full((1, D_CONV)), full((1, D_CONV)), full((1, D_CONV)),
                  full((1, D_ATTN)), full((1, D_CONV)), full((D_MODEL, D_MODEL)), full((1, D_MODEL))],
        out_specs=[row(D_MODEL), pl.BlockSpec((D_MODEL, tile), lambda b, i: (0, b * nt + i))],
        out_shape=[jax.ShapeDtypeStruct((B, S, D_MODEL), F32),
                   jax.ShapeDtypeStruct((D_MODEL, B * S), BF16)],
        scratch_shapes=[pltpu.VMEM((CONV_HALO + tile, D_CONV), F32),
                        pltpu.VMEM((SUB - 1, CONV_HALO + tile - SUB, D_CONV), F32)],
        compiler_params=_params(("parallel", "parallel")),
        name="mix_out",
    )(x, attn, glu, glu, conv_w, conv_b, ln_g, ln_b, g_attn, g_conv, w_out, g_ffn)


def _sorting_network(n):
    pairs, p = [], 1
    while p < n:
        k = p
        while k >= 1:
            for j in range(k % p, n - k, 2 * k):
                for i in range(min(k, n - j - k)):
                    if (i + j) // (2 * p) == (i + j + k) // (2 * p):
                        pairs.append((i + j, i + j + k))
            k //= 2
        p *= 2
    return pairs


def _sort_desc(v, pairs):
    v = list(v)
    for i, j in pairs:
        v[i], v[j] = jnp.maximum(v[i], v[j]), jnp.minimum(v[i], v[j])
    return v


def _bitonic_sort_desc(v):
    v = list(v)
    d = len(v) // 2
    while d >= 1:
        for i in range(len(v)):
            if i & d == 0:
                v[i], v[i + d] = jnp.maximum(v[i], v[i + d]), jnp.minimum(v[i], v[i + d])
        d //= 2
    return v


def _merge_sublanes(v, shift):
    n = len(v)
    return _bitonic_sort_desc([jnp.maximum(v[k], pltpu.roll(v[n - 1 - k], shift, 0)) for k in range(n)])


def _top16_over_rows(tiles):
    v = _sort_desc(tiles, _sorting_network(len(tiles)))
    for shift in (4, 2, 1):
        v = _merge_sublanes(v, shift)
    return v


def _peer_select_kernel(xt_ref, wq_ref, keys_ref, w1_ref, c_ref, w2_ref, r2_ref, q_ref):
    lt = pl.program_id(1)
    n_lt = q_ref.shape[0]

    @pl.when(lt == 0)
    def _():
        q = jnp.dot(wq_ref[...], xt_ref[...], preferred_element_type=F32).astype(BF16)
        for k in range(n_lt):
            q_ref[k] = q[:, k * LANES:(k + 1) * LANES]

    sub = lax.broadcasted_iota(jnp.int32, (SUB, LANES), 0)
    groups = N_KEYS // SUB

    def compact(v, lo):
        out = v[lo + SUB - 1]
        for s in range(SUB - 2, -1, -1):
            out = jnp.where(sub == s, v[lo + s], out)
        return out

    def head(h, carry):
        base = pl.multiple_of(h * 2 * N_KEYS, 2 * N_KEYS)
        s1 = jnp.dot(keys_ref[0], q_ref[lt, pl.ds(base, N_KEYS), :], preferred_element_type=F32)
        s2 = jnp.dot(keys_ref[1], q_ref[lt, pl.ds(base + N_KEYS, N_KEYS), :], preferred_element_type=F32)
        s1 = [s1[g * SUB:(g + 1) * SUB, :] for g in range(groups)]
        s2 = [s2[g * SUB:(g + 1) * SUB, :] for g in range(groups)]
        a = _top16_over_rows(s1)
        b = _top16_over_rows(s2)
        a_lo, a_hi, b_lo, b_hi = compact(a, 0), compact(a, SUB), compact(b, 0), compact(b, SUB)
        neg = jnp.full((SUB, LANES), NEG_INF, F32)
        c_j2 = jnp.where(sub < 5, a[2] + b_lo, neg)
        c_j3 = jnp.where(sub < 4, a[3] + b_lo, neg)
        c_j4 = jnp.where(sub < 3, a[4] + b_lo, neg)
        c_k0 = jnp.where(sub >= 5, a_lo + b[0], neg)
        c_k1 = jnp.where(sub >= 5, a_lo + b[1], neg)
        cands = [a[0] + b_lo, a[0] + b_hi, a[1] + b_lo,
                 jnp.where(sub < 5, c_j2, pltpu.roll(c_j4, 5, 0)),
                 jnp.where(sub < 4, c_j3, c_k0), c_k1, a_hi + b[0], neg]
        v = _sort_desc(cands, _sorting_network(len(cands)))
        v = _bitonic_sort_desc(v + [pltpu.roll(t, 4, 0) for t in reversed(v)])
        for shift in (2, 1):
            v = _merge_sublanes(v, shift)
        tau = v[PEER_TOPK - 1]
        top = a[0] + b[0]
        zsum = None
        for c in cands[:-1]:
            term = jnp.where(c >= tau, jnp.exp(c - top), 0.0)
            zsum = term if zsum is None else zsum + term
        for shift in (4, 2, 1):
            zsum = zsum + pltpu.roll(zsum, shift, 0)
        scale = 0.5 / zsum
        extra = []
        for j, kmax in ((0, 15), (1, 7), (2, 4)):
            e = jnp.zeros((SUB, LANES), F32)
            for k in range(4, kmax + 1):
                e = jnp.where(a[j] + b[k] >= tau, float(k - 3), e)
            extra.append(e)
        cnt, rank, w1, w2 = [], [], [], []
        for g in range(groups):
            c = jnp.zeros((SUB, LANES), F32)
            for k in range(4):
                c = jnp.where(s1[g] + b[k] >= tau, float(k + 1), c)
            c = c + jnp.where(s1[g] == a[0], extra[0],
                              jnp.where(s1[g] == a[1], extra[1], jnp.where(s1[g] == a[2], extra[2], 0.0)))
            r = jnp.zeros((SUB, LANES), F32)
            for k in range(PEER_TOPK):
                r = jnp.where(b[k] > s2[g], float(k + 1), r)
            cnt.append(c)
            rank.append(r)
            w1.append(jnp.exp(s1[g] - a[0]) * scale)
            w2.append(jnp.exp(s2[g] - b[0]))
        w1_ref[h] = jnp.concatenate(w1, axis=0)
        c_ref[h] = jnp.concatenate(cnt, axis=0)
        w2_ref[h] = jnp.concatenate(w2, axis=0).astype(BF16)
        r2_ref[h] = jnp.concatenate(rank, axis=0).astype(BF16)
        return carry

    def head_pair(i, carry):
        head(2 * i, carry)
        return head(2 * i + 1, carry)

    lax.fori_loop(0, PEER_HEADS // 2, head_pair, 0)


def _peer_select(xnt, wq_t, keys, tile):
    n = xnt.shape[1]
    n_lt = tile // LANES
    tab = pl.BlockSpec((PEER_HEADS, N_KEYS, LANES), lambda i, lt: (0, 0, i * n_lt + lt))
    return pl.pallas_call(
        _peer_select_kernel,
        grid=(n // tile, n_lt),
        in_specs=[pl.BlockSpec((D_MODEL, tile), lambda i, lt: (0, i)),
                  pl.BlockSpec(wq_t.shape, lambda i, lt: (0, 0)),
                  pl.BlockSpec(keys.shape, lambda i, lt: (0, 0, 0))],
        out_specs=[tab, tab, tab, tab],
        out_shape=[jax.ShapeDtypeStruct((PEER_HEADS, N_KEYS, n), F32)] * 2
        + [jax.ShapeDtypeStruct((PEER_HEADS, N_KEYS, n), BF16)] * 2,
        scratch_shapes=[pltpu.VMEM((n_lt, PEER_HEADS * 2 * N_KEYS, LANES), BF16)],
        compiler_params=_params(("parallel", "arbitrary")),
        name="peer_select",
    )(xnt, wq_t, keys)


def _peer_dense_kernel(xt_ref, u_ref, vt_ref, w1_ref, c_ref, w2_ref, r2_ref, h_ref, o_ref,
                       hid_ref, act_ref, acc_ref):
    j = pl.program_id(1)
    tile = xt_ref.shape[1]
    rows_per_chunk = u_ref.shape[0] // N_KEYS

    @pl.when(j == 0)
    def _():
        acc_ref[...] = jnp.zeros_like(acc_ref)

    hid_ref[...] = jnp.dot(u_ref[...], xt_ref[...], preferred_element_type=F32)
    for row in range(rows_per_chunk):
        for lp in range(tile // PAIR):
            lanes = slice(lp * PAIR, (lp + 1) * PAIR)

            def bcast(ref, h):
                parts = [jnp.broadcast_to(ref[h, row:row + 1, lp * PAIR + k * LANES:lp * PAIR + (k + 1) * LANES],
                                          (SUB, LANES)) for k in range(PAIR // LANES)]
                return jnp.concatenate(parts, axis=1).astype(BF16)

            gate = [jnp.zeros((SUB, PAIR), BF16) for _ in range(N_KEYS // SUB)]
            for h in range(PEER_HEADS):
                cnt = bcast(c_ref, h)
                w1 = bcast(w1_ref, h)
                for v in range(N_KEYS // SUB):
                    keys = slice(v * SUB, (v + 1) * SUB)
                    sel = jnp.where(r2_ref[h, keys, lanes] < cnt, w2_ref[h, keys, lanes],
                                    jnp.zeros((SUB, PAIR), BF16))
                    gate[v] = gate[v] + sel * w1
            for v in range(N_KEYS // SUB):
                rows = slice(row * N_KEYS + v * SUB, row * N_KEYS + (v + 1) * SUB)
                hid = hid_ref[rows, lanes]
                gelu2 = hid + hid * lax.erf(hid * math.sqrt(0.5))
                act_ref[rows, lanes] = gelu2.astype(BF16) * gate[v]
    acc_ref[...] += jnp.dot(vt_ref[...], act_ref[...], preferred_element_type=F32)

    @pl.when(j == pl.num_programs(1) - 1)
    def _():
        o_ref[...] = h_ref[...] + acc_ref[...].T


def _peer_dense(xnt, u, vt, w1, cnt, w2, r2, h, tile, chunk):
    n = xnt.shape[1]
    rows_per_chunk = chunk // N_KEYS
    small = pl.BlockSpec((PEER_HEADS, rows_per_chunk, tile), lambda i, j: (0, j, i))
    big = pl.BlockSpec((PEER_HEADS, N_KEYS, tile), lambda i, j: (0, 0, i))
    res = pl.BlockSpec((tile, D_MODEL), lambda i, j: (i, 0))
    return pl.pallas_call(
        _peer_dense_kernel,
        grid=(n // tile, N_EXPERTS // chunk),
        in_specs=[pl.BlockSpec((D_MODEL, tile), lambda i, j: (0, i)),
                  pl.BlockSpec((chunk, D_MODEL), lambda i, j: (j, 0)),
                  pl.BlockSpec((D_MODEL, chunk), lambda i, j: (0, j)),
                  small, small, big, big, res],
        out_specs=res,
        out_shape=jax.ShapeDtypeStruct((n, D_MODEL), F32),
        scratch_shapes=[pltpu.VMEM((chunk, tile), F32), pltpu.VMEM((chunk, tile), BF16),
                        pltpu.VMEM((D_MODEL, tile), F32)],
        compiler_params=_params(("parallel", "arbitrary")),
        name="peer_dense",
    )(xnt, u, vt, w1, cnt, w2, r2, h)


def _ple_final_kernel(h_ref, p_ref, gple_ref, wg_ref, wp_ref, gfin_ref, o_ref, *, last):
    h = h_ref[...]
    gate = jax.nn.sigmoid(jnp.dot(_rms(h, gple_ref[...]).astype(BF16), wg_ref[...],
                                  preferred_element_type=F32))
    proj = jnp.dot(p_ref[...].astype(BF16), wp_ref[...], preferred_element_type=F32)
    h = h + gate * proj
    o_ref[...] = _rms(h, gfin_ref[...]) if last else h


def _ple_final(h, p, g_ple, w_gate, w_proj, g_final, tile, last):
    n = h.shape[0]
    full = lambda shape: pl.BlockSpec(shape, lambda i: (0,) * len(shape))
    return pl.pallas_call(
        functools.partial(_ple_final_kernel, last=last),
        grid=(n // tile,),
        in_specs=[pl.BlockSpec((tile, D_MODEL), lambda i: (i, 0)),
                  pl.BlockSpec((tile, PLE_DIM), lambda i: (i, 0)),
                  full((1, D_MODEL)), full((D_MODEL, D_MODEL)), full((PLE_DIM, D_MODEL)),
                  full((1, D_MODEL))],
        out_specs=pl.BlockSpec((tile, D_MODEL), lambda i: (i, 0)),
        out_shape=jax.ShapeDtypeStruct((n, D_MODEL), F32),
        compiler_params=_params(("parallel",)),
        name="ple_final",
    )(h, p, g_ple, w_gate, w_proj, g_final)


def _rope_tables(positions):
    half = ROPE_DIM // 2
    dim = jnp.arange(LANES) % HEAD_DIM
    inv_freq = jnp.where(dim < ROPE_DIM, ROPE_THETA ** (-2.0 * (dim % half).astype(F32) / ROPE_DIM), 0.0)
    ang = positions.astype(F32)[..., None] * inv_freq
    return jnp.cos(ang), jnp.sin(ang)


def kernel(x, p, positions, norm_mix, w_in, conv_w, conv_b, conv_ln_g, conv_ln_b, g_attn_out,
           g_conv_out, w_out, norm_ffn, peer_wq, sub_keys, expert_u, expert_v, norm_ple,
           w_ple_gate, w_ple_proj, final_norm):
    B, S, _ = x.shape
    depth = w_in.shape[0]
    n = B * S
    cos_t, sin_t = _rope_tables(positions)
    super_block = ATTN_BLOCK * max(d for _, d in DILATED_PATTERNS)
    vec = lambda t: t.reshape(1, -1)
    h = x
    for i in range(depth):
        q, k, v, glu = _mix_in(h, vec(norm_mix[i]), w_in[i].astype(BF16), cos_t, sin_t, tile=512)
        acc = None
        for window, dilation in sorted(DILATED_PATTERNS, key=lambda wd: -wd[1]):
            acc = _attn_pattern(q, k, v, acc, window, dilation, super_block)
        h, xnt = _mix_out(h, acc[0], glu, conv_w[i], vec(conv_b[i]), vec(conv_ln_g[i]),
                          vec(conv_ln_b[i]), vec(g_attn_out[i]), vec(g_conv_out[i]),
                          w_out[i].astype(BF16), vec(norm_ffn[i]), tile=256)
        w1, cnt, w2, r2 = _peer_select(xnt, peer_wq[i].T.astype(BF16), sub_keys[i].astype(BF16),
                                       tile=256)
        h = _peer_dense(xnt, expert_u[i].astype(BF16), expert_v[i].T.astype(BF16), w1, cnt, w2, r2,
                        h.reshape(n, D_MODEL), tile=512, chunk=1024)
        h = _ple_final(h, p[i].reshape(n, PLE_DIM), vec(norm_ple[i]), w_ple_gate[i].astype(BF16),
                       w_ple_proj[i].astype(BF16), vec(final_norm), tile=512, last=(i == depth - 1))
        h = h.reshape(B, S, D_MODEL)
    return h
```

```python
import functools
import math

import jax
import jax.numpy as jnp
from jax import lax
from jax.experimental import pallas as pl
from jax.experimental.pallas import tpu as pltpu

F32 = jnp.float32
BF16 = jnp.bfloat16

D_MODEL = 1024
PLE_DIM = 256
HEAD_DIM = 64
N_ATTN_HEADS = 8
D_ATTN = N_ATTN_HEADS * HEAD_DIM
D_CONV = D_MODEL // 2
D_IN = 3 * D_ATTN + 2 * D_CONV
CONV_WIDTH = 31
ROPE_DIM = HEAD_DIM // 4
ROPE_THETA = 500000.0
DILATED_PATTERNS = ((128, 1), (512, 4), (2048, 16))
ATTN_BLOCK = 128
N_KEYS = 128
N_EXPERTS = N_KEYS * N_KEYS
PEER_HEADS = 8
PEER_TOPK = 16
EPS = 1e-6

LANES = 128
SUB = 8
PAIR = 2 * LANES
HEADS_PER_LANE_TILE = LANES // HEAD_DIM
CONV_HALO = 32
VMEM_LIMIT = 56 * 1024 * 1024

NEG_INF = float("-inf")


def _params(sem):
    return pltpu.CompilerParams(dimension_semantics=sem, vmem_limit_bytes=VMEM_LIMIT)


def _rms(x, g):
    return x * lax.rsqrt(jnp.mean(x * x, axis=-1, keepdims=True) + EPS) * g


def _mix_in_kernel(x_ref, g_ref, w_ref, cos_ref, sin_ref, q_ref, k_ref, v_ref, glu_ref):
    xn = _rms(x_ref[...], g_ref[...])
    z = jnp.dot(xn.astype(BF16), w_ref[...], preferred_element_type=F32)
    half = ROPE_DIM // 2
    cos, sin = cos_ref[...], sin_ref[...]
    dim = lax.broadcasted_iota(jnp.int32, (1, LANES), 1) % HEAD_DIM
    sa = jnp.where(dim < half, -sin, 0.0)
    sb = jnp.where((dim >= half) & (dim < ROPE_DIM), sin, 0.0)

    def rope(tt):
        return tt * cos + pltpu.roll(tt, LANES - half, 1) * sa + pltpu.roll(tt, half, 1) * sb

    for i in range(D_ATTN // LANES):
        q_ref[i] = rope(z[:, i * LANES:(i + 1) * LANES])
        k_ref[i] = rope(z[:, D_ATTN + i * LANES:D_ATTN + (i + 1) * LANES])
        v_ref[i] = z[:, 2 * D_ATTN + i * LANES:2 * D_ATTN + (i + 1) * LANES]
    a = z[:, 3 * D_ATTN:3 * D_ATTN + D_CONV]
    gt = z[:, 3 * D_ATTN + D_CONV:]
    glu_ref[...] = a * jax.nn.sigmoid(gt)


def _mix_in(x, g, w_in, cos_t, sin_t, tile):
    B, S, _ = x.shape
    n_slab = D_ATTN // LANES
    row = lambda d: pl.BlockSpec((None, tile, d), lambda b, i: (b, i, 0))
    full = lambda shape: pl.BlockSpec(shape, lambda b, i: (0,) * len(shape))
    slab = pl.BlockSpec((None, n_slab, tile, LANES), lambda b, i: (b, 0, i, 0))
    return pl.pallas_call(
        _mix_in_kernel,
        grid=(B, S // tile),
        in_specs=[row(D_MODEL), full((1, D_MODEL)), full((D_MODEL, D_IN)), row(LANES), row(LANES)],
        out_specs=[slab, slab, slab, row(D_CONV)],
        out_shape=[jax.ShapeDtypeStruct((B, n_slab, S, LANES), F32)] * 3
        + [jax.ShapeDtypeStruct((B, S, D_CONV), F32)],
        compiler_params=_params(("parallel", "parallel")),
        name="mix_in",
    )(x, g, w_in, cos_t, sin_t)


COARSE_STRIDE = 4


def _attn_kernel(q_ref, kc_ref, vc_ref, kp_ref, vp_ref, o_ref, oacc_ref, lacc_ref, *stage, patterns):
    n = pl.program_id(2)
    blk = ATTN_BLOCK
    rows_blk = q_ref.shape[0]
    lane = lax.broadcasted_iota(jnp.int32, (1, LANES), 1)
    head0 = lane < HEAD_DIM
    scale = HEAD_DIM ** -0.5
    qscale = [jnp.where(head0, scale, 0.0).astype(BF16), jnp.where(head0, 0.0, scale).astype(BF16)]
    qi = lax.broadcasted_iota(jnp.int32, (blk, 2 * blk), 0)
    kj = lax.broadcasted_iota(jnp.int32, (blk, 2 * blk), 1)
    dist = blk + qi - kj
    seq_start = (kj >= blk) | (n > 0)

    for index, (window, dil) in enumerate(patterns):
        first = index == 0
        band = (dist >= 0) & (dist <= window // dil)
        band_first = band & seq_start
        coarse = COARSE_STRIDE if dil > COARSE_STRIDE else 1
        fine = dil // coarse
        prev_skip = rows_blk - blk * dil
        srcs = {"q": q_ref, "kc": kc_ref, "vc": vc_ref, "kp": kp_ref, "vp": vp_ref,
                "o": oacc_ref, "l": lacc_ref}
        if coarse > 1:
            assert first and dil % coarse == 0 and prev_skip % coarse == 0
            for name, dst in zip(("q", "kc", "vc", "kp", "vp"), stage[:5]):
                for p in range(coarse):
                    dst[p] = srcs[name][pl.ds(p, rows_blk // coarse, stride=coarse), :]
                srcs[name] = dst
            srcs["o"], srcs["l"] = stage[5], stage[6]

        def view(name, r, first_elem, count):
            skip = prev_skip if name in ("kp", "vp") else 0
            if coarse > 1:
                return srcs[name], (r % coarse, pl.ds(skip // coarse + r // coarse + fine * first_elem, count,
                                                       stride=fine), slice(None))
            if dil > 1:
                return srcs[name], (pl.ds(skip + r + dil * first_elem, count, stride=dil), slice(None))
            return srcs[name], (pl.ds(skip + first_elem, count), slice(None))

        def take(name, r, first_elem, count):
            ref, idx = view(name, r, first_elem, count)
            return ref[idx]

        def put(name, r, first_elem, value):
            ref, idx = view(name, r, first_elem, value.shape[0])
            ref[idx] = value

        for r in range(dil):
            for qb in range(rows_blk // (blk * dil)):
                mask = band_first if qb == 0 else band
                q2 = take("q", r, qb * blk, blk).astype(BF16)
                if qb == 0:
                    k2 = jnp.concatenate([take("kp", r, 0, blk), take("kc", r, 0, blk)], axis=0)
                    v2 = jnp.concatenate([take("vp", r, 0, blk), take("vc", r, 0, blk)], axis=0)
                else:
                    k2 = take("kc", r, (qb - 1) * blk, 2 * blk)
                    v2 = take("vc", r, (qb - 1) * blk, 2 * blk)
                k2, v2 = k2.astype(BF16), v2.astype(BF16)
                o_h, lse_h = [], []
                for hh in range(HEADS_PER_LANE_TILE):
                    s = lax.dot_general(q2 * qscale[hh], k2, (((1,), (1,)), ((), ())),
                                        preferred_element_type=F32)
                    s = jnp.where(mask, s, NEG_INF)
                    m = jnp.max(s, axis=-1, keepdims=True)
                    e = jnp.exp(s - m)
                    l = jnp.sum(e, axis=-1, keepdims=True)
                    o_h.append(jnp.dot(e.astype(BF16), v2, preferred_element_type=F32) / l)
                    lse_h.append(m + jnp.log(l))
                o_new = jnp.where(head0, o_h[0], o_h[1])
                lse_new = jnp.where(head0, lse_h[0], lse_h[1])
                if not first:
                    lse_old = take("l", r, qb * blk, blk)
                    mx = jnp.maximum(lse_old, lse_new)
                    w_old = jnp.exp(lse_old - mx)
                    w_new = jnp.exp(lse_new - mx)
                    tot = w_old + w_new
                    o_new = (take("o", r, qb * blk, blk) * w_old + o_new * w_new) / tot
                    lse_new = mx + jnp.log(tot)
                put("o", r, qb * blk, o_new)
                put("l", r, qb * blk, lse_new)

        if coarse > 1:
            for src, dst in ((stage[5], oacc_ref), (stage[6], lacc_ref)):
                for p in range(coarse):
                    dst[pl.ds(p, rows_blk // coarse, stride=coarse), :] = src[p]

    o_ref[...] = oacc_ref[...]


def _attention(q, k, v, patterns, super_block):
    B, n_slab, S, _ = q.shape
    assert S % super_block == 0 and all(super_block % (ATTN_BLOCK * d) == 0 for _, d in patterns)
    cur = pl.BlockSpec((None, None, super_block, LANES), lambda b, hp, n: (b, hp, n, 0))
    prev = pl.BlockSpec((None, None, super_block, LANES), lambda b, hp, n: (b, hp, jnp.maximum(n - 1, 0), 0))
    slab = pltpu.VMEM((super_block, LANES), F32)
    staged = pltpu.VMEM((COARSE_STRIDE, super_block // COARSE_STRIDE, LANES), F32)
    needs_stage = patterns[0][1] > COARSE_STRIDE
    return pl.pallas_call(
        functools.partial(_attn_kernel, patterns=tuple(patterns)),
        grid=(B, n_slab, S // super_block),
        in_specs=[cur, cur, cur, prev, prev],
        out_specs=cur,
        out_shape=jax.ShapeDtypeStruct(q.shape, F32),
        scratch_shapes=[slab, slab] + [staged] * (7 if needs_stage else 0),
        compiler_params=_params(("parallel", "parallel", "parallel")),
        name="attention",
    )(q, k, v, k, v)


def _mix_out_kernel(x_ref, attn_ref, glu_ref, halo_ref, cw_ref, cb_ref, lng_ref, lnb_ref,
                    ga_ref, gc_ref, wo_ref, gffn_ref, h_ref, xnt_ref, win_ref, rot_ref, *, tile):
    i = pl.program_id(1)
    halo = halo_ref[...]
    win_ref[0:CONV_HALO, :] = jnp.where(i > 0, halo, jnp.zeros_like(halo))
    win_ref[CONV_HALO:, :] = glu_ref[...]
    span = rot_ref.shape[1]
    for shift in range(1, SUB):
        rot_ref[shift - 1] = win_ref[pl.ds(shift, span), :]
    off = CONV_HALO - (CONV_WIDTH - 1)
    c = jnp.zeros((tile, D_CONV), F32)
    for j in range(CONV_WIDTH):
        shift, base = (off + j) % SUB, (off + j) // SUB * SUB
        rows = win_ref[base:base + tile, :] if shift == 0 else rot_ref[shift - 1, base:base + tile, :]
        c = c + rows * cw_ref[j:j + 1, :]
    c = c + cb_ref[...]
    mu = jnp.mean(c, axis=-1, keepdims=True)
    var = jnp.mean(jnp.square(c - mu), axis=-1, keepdims=True)
    c = (c - mu) * lax.rsqrt(var + EPS) * lng_ref[...] + lnb_ref[...]
    c = c * jax.nn.sigmoid(c)
    attn = jnp.concatenate([attn_ref[s] for s in range(D_ATTN // LANES)], axis=1)
    y = jnp.dot(_rms(attn, ga_ref[...]).astype(BF16), wo_ref[0:D_ATTN, :],
                preferred_element_type=F32)
    y = y + jnp.dot(_rms(c, gc_ref[...]).astype(BF16), wo_ref[D_ATTN:, :],
                    preferred_element_type=F32)
    h = x_ref[...] + y
    h_ref[...] = h
    xnt_ref[...] = _rms(h, gffn_ref[...]).T.astype(BF16)


def _mix_out(x, attn, glu, conv_w, conv_b, ln_g, ln_b, g_attn, g_conv, w_out, g_ffn, tile):
    B, S, _ = x.shape
    nt = S // tile
    hpt = tile // CONV_HALO
    row = lambda d: pl.BlockSpec((None, tile, d), lambda b, i: (b, i, 0))
    full = lambda shape: pl.BlockSpec(shape, lambda b, i: (0,) * len(shape))
    halo = pl.BlockSpec((None, CONV_HALO, D_CONV), lambda b, i: (b, jnp.maximum(i * hpt - 1, 0), 0))
    return pl.pallas_call(
        functools.partial(_mix_out_kernel, tile=tile),
        grid=(B, nt),
        in_specs=[row(D_MODEL),
                  pl.BlockSpec((None, D_ATTN // LANES, tile, LANES), lambda b, i: (b, 0, i, 0)),
                  row(D_CONV), halo,
                  full((CONV_WIDTH, D_CONV)), full((1, D_CONV)), full((1, D_CONV)), full((1, D_CONV)),
                  full((1, D_ATTN)), full((1, D_CONV)), full((D_MODEL, D_MODEL)), full((1, D_MODEL))],
        out_specs=[row(D_MODEL), pl.BlockSpec((D_MODEL, tile), lambda b, i: (0, b * nt + i))],
        out_shape=[jax.ShapeDtypeStruct((B, S, D_MODEL), F32),
                   jax.ShapeDtypeStruct((D_MODEL, B * S), BF16)],
        scratch_shapes=[pltpu.VMEM((CONV_HALO + tile, D_CONV), F32),
                        pltpu.VMEM((SUB - 1, CONV_HALO + tile - SUB, D_CONV), F32)],
        compiler_params=_params(("parallel", "parallel")),
        name="mix_out",
    )(x, attn, glu, glu, conv_w, conv_b, ln_g, ln_b, g_attn, g_conv, w_out, g_ffn)


def _sorting_network(n):
    pairs, p = [], 1
    while p < n:
        k = p
        while k >= 1:
            for j in range(k % p, n - k, 2 * k):
                for i in range(min(k, n - j - k)):
                    if (i + j) // (2 * p) == (i + j + k) // (2 * p):
                        pairs.append((i + j, i + j + k))
            k //= 2
        p *= 2
    return pairs


def _sort_desc(v, pairs):
    v = list(v)
    for i, j in pairs:
        v[i], v[j] = jnp.maximum(v[i], v[j]), jnp.minimum(v[i], v[j])
    return v


def _bitonic_sort_desc(v):
    v = list(v)
    d = len(v) // 2
    while d >= 1:
        for i in range(len(v)):
            if i & d == 0:
                v[i], v[i + d] = jnp.maximum(v[i], v[i + d]), jnp.minimum(v[i], v[i + d])
        d //= 2
    return v


def _merge_sublanes(v, shift):
    n = len(v)
    return _bitonic_sort_desc([jnp.maximum(v[k], pltpu.roll(v[n - 1 - k], shift, 0)) for k in range(n)])


def _top16_over_rows(tiles):
    v = _sort_desc(tiles, _sorting_network(len(tiles)))
    for shift in (4, 2, 1):
        v = _merge_sublanes(v, shift)
    return v


def _peer_select_kernel(xt_ref, wq_ref, keys_ref, w1_ref, c_ref, w2_ref, r2_ref, q_ref):
    lt = pl.program_id(1)
    n_lt = q_ref.shape[0]

    @pl.when(lt == 0)
    def _():
        q = jnp.dot(wq_ref[...], xt_ref[...], preferred_element_type=F32).astype(BF16)
        for k in range(n_lt):
            q_ref[k] = q[:, k * LANES:(k + 1) * LANES]

    sub = lax.broadcasted_iota(jnp.int32, (SUB, LANES), 0)
    groups = N_KEYS // SUB

    def compact(v, lo):
        out = v[lo + SUB - 1]
        for s in range(SUB - 2, -1, -1):
            out = jnp.where(sub == s, v[lo + s], out)
        return out

    def head(h, carry):
        base = pl.multiple_of(h * 2 * N_KEYS, 2 * N_KEYS)
        s1 = jnp.dot(keys_ref[0], q_ref[lt, pl.ds(base, N_KEYS), :], preferred_element_type=F32)
        s2 = jnp.dot(keys_ref[1], q_ref[lt, pl.ds(base + N_KEYS, N_KEYS), :], preferred_element_type=F32)
        s1 = [s1[g * SUB:(g + 1) * SUB, :] for g in range(groups)]
        s2 = [s2[g * SUB:(g + 1) * SUB, :] for g in range(groups)]
        a = _top16_over_rows(s1)
        b = _top16_over_rows(s2)
        a_lo, a_hi, b_lo, b_hi = compact(a, 0), compact(a, SUB), compact(b, 0), compact(b, SUB)
        neg = jnp.full((SUB, LANES), NEG_INF, F32)
        c_j2 = jnp.where(sub < 5, a[2] + b_lo, neg)
        c_j3 = jnp.where(sub < 4, a[3] + b_lo, neg)
        c_j4 = jnp.where(sub < 3, a[4] + b_lo, neg)
        c_k0 = jnp.where(sub >= 5, a_lo + b[0], neg)
        c_k1 = jnp.where(sub >= 5, a_lo + b[1], neg)
        cands = [a[0] + b_lo, a[0] + b_hi, a[1] + b_lo,
                 jnp.where(sub < 5, c_j2, pltpu.roll(c_j4, 5, 0)),
                 jnp.where(sub < 4, c_j3, c_k0), c_k1, a_hi + b[0], neg]
        v = _sort_desc(cands, _sorting_network(len(cands)))
        v = _bitonic_sort_desc(v + [pltpu.roll(t, 4, 0) for t in reversed(v)])
        for shift in (2, 1):
            v = _merge_sublanes(v, shift)
        tau = v[PEER_TOPK - 1]
        top = a[0] + b[0]
        zsum = None
        for c in cands[:-1]:
            term = jnp.where(c >= tau, jnp.exp(c - top), 0.0)
            zsum = term if zsum is None else zsum + term
        for shift in (4, 2, 1):
            zsum = zsum + pltpu.roll(zsum, shift, 0)
        scale = 0.5 / zsum
        extra = []
        for j, kmax in ((0, 15), (1, 7), (2, 4)):
            e = jnp.zeros((SUB, LANES), F32)
            for k in range(4, kmax + 1):
                e = jnp.where(a[j] + b[k] >= tau, float(k - 3), e)
            extra.append(e)
        cnt, rank, w1, w2 = [], [], [], []
        for g in range(groups):
            c = jnp.zeros((SUB, LANES), F32)
            for k in range(4):
                c = jnp.where(s1[g] + b[k] >= tau, float(k + 1), c)
            c = c + jnp.where(s1[g] == a[0], extra[0],
                              jnp.where(s1[g] == a[1], extra[1], jnp.where(s1[g] == a[2], extra[2], 0.0)))
            r = jnp.zeros((SUB, LANES), F32)
            for k in range(PEER_TOPK):
                r = jnp.where(b[k] > s2[g], float(k + 1), r)
            cnt.append(c)
            rank.append(r)
            w1.append(jnp.exp(s1[g] - a[0]) * scale)
            w2.append(jnp.exp(s2[g] - b[0]))
        w1_ref[h] = jnp.concatenate(w1, axis=0)
        c_ref[h] = jnp.concatenate(cnt, axis=0)
        w2_ref[h] = jnp.concatenate(w2, axis=0).astype(BF16)
        r2_ref[h] = jnp.concatenate(rank, axis=0).astype(BF16)
        return carry

    def head_pair(i, carry):
        head(2 * i, carry)
        return head(2 * i + 1, carry)

    lax.fori_loop(0, PEER_HEADS // 2, head_pair, 0)


def _peer_select(xnt, wq_t, keys, tile):
    n = xnt.shape[1]
    n_lt = tile // LANES
    tab = pl.BlockSpec((PEER_HEADS, N_KEYS, LANES), lambda i, lt: (0, 0, i * n_lt + lt))
    return pl.pallas_call(
        _peer_select_kernel,
        grid=(n // tile, n_lt),
        in_specs=[pl.BlockSpec((D_MODEL, tile), lambda i, lt: (0, i)),
                  pl.BlockSpec(wq_t.shape, lambda i, lt: (0, 0)),
                  pl.BlockSpec(keys.shape, lambda i, lt: (0, 0, 0))],
        out_specs=[tab, tab, tab, tab],
        out_shape=[jax.ShapeDtypeStruct((PEER_HEADS, N_KEYS, n), F32)] * 2
        + [jax.ShapeDtypeStruct((PEER_HEADS, N_KEYS, n), BF16)] * 2,
        scratch_shapes=[pltpu.VMEM((n_lt, PEER_HEADS * 2 * N_KEYS, LANES), BF16)],
        compiler_params=_params(("parallel", "arbitrary")),
        name="peer_select",
    )(xnt, wq_t, keys)


def _peer_dense_kernel(xt_ref, u_ref, vt_ref, w1_ref, c_ref, w2_ref, r2_ref, h_ref, o_ref,
                       hid_ref, act_ref, acc_ref):
    j = pl.program_id(1)
    tile = xt_ref.shape[1]
    rows_per_chunk = u_ref.shape[0] // N_KEYS

    @pl.when(j == 0)
    def _():
        acc_ref[...] = jnp.zeros_like(acc_ref)

    hid_ref[...] = jnp.dot(u_ref[...], xt_ref[...], preferred_element_type=F32)
    for row in range(rows_per_chunk):
        for lp in range(tile // PAIR):
            lanes = slice(lp * PAIR, (lp + 1) * PAIR)

            def bcast(ref, h):
                parts = [jnp.broadcast_to(ref[h, row:row + 1, lp * PAIR + k * LANES:lp * PAIR + (k + 1) * LANES],
                                          (SUB, LANES)) for k in range(PAIR // LANES)]
                return jnp.concatenate(parts, axis=1).astype(BF16)

            gate = [jnp.zeros((SUB, PAIR), BF16) for _ in range(N_KEYS // SUB)]
            for h in range(PEER_HEADS):
                cnt = bcast(c_ref, h)
                w1 = bcast(w1_ref, h)
                for v in range(N_KEYS // SUB):
                    keys = slice(v * SUB, (v + 1) * SUB)
                    sel = jnp.where(r2_ref[h, keys, lanes] < cnt, w2_ref[h, keys, lanes],
                                    jnp.zeros((SUB, PAIR), BF16))
                    gate[v] = gate[v] + sel * w1
            for v in range(N_KEYS // SUB):
                rows = slice(row * N_KEYS + v * SUB, row * N_KEYS + (v + 1) * SUB)
                hid = hid_ref[rows, lanes]
                gelu2 = hid + hid * lax.erf(hid * math.sqrt(0.5))
                act_ref[rows, lanes] = gelu2.astype(BF16) * gate[v]
    acc_ref[...] += jnp.dot(vt_ref[...], act_ref[...], preferred_element_type=F32)

    @pl.when(j == pl.num_programs(1) - 1)
    def _():
        o_ref[...] = h_ref[...] + acc_ref[...].T


def _peer_dense(xnt, u, vt, w1, cnt, w2, r2, h, tile, chunk):
    n = xnt.shape[1]
    rows_per_chunk = chunk // N_KEYS
    small = pl.BlockSpec((PEER_HEADS, rows_per_chunk, tile), lambda i, j: (0, j, i))
    big = pl.BlockSpec((PEER_HEADS, N_KEYS, tile), lambda i, j: (0, 0, i))
    res = pl.BlockSpec((tile, D_MODEL), lambda i, j: (i, 0))
    return pl.pallas_call(
        _peer_dense_kernel,
        grid=(n // tile, N_EXPERTS // chunk),
        in_specs=[pl.BlockSpec((D_MODEL, tile), lambda i, j: (0, i)),
                  pl.BlockSpec((chunk, D_MODEL), lambda i, j: (j, 0)),
                  pl.BlockSpec((D_MODEL, chunk), lambda i, j: (0, j)),
                  small, small, big, big, res],
        out_specs=res,
        out_shape=jax.ShapeDtypeStruct((n, D_MODEL), F32),
        scratch_shapes=[pltpu.VMEM((chunk, tile), F32), pltpu.VMEM((chunk, tile), BF16),
                        pltpu.VMEM((D_MODEL, tile), F32)],
        compiler_params=_params(("parallel", "arbitrary")),
        name="peer_dense",
    )(xnt, u, vt, w1, cnt, w2, r2, h)


def _ple_final_kernel(h_ref, p_ref, gple_ref, wg_ref, wp_ref, gfin_ref, o_ref, *, last):
    h = h_ref[...]
    gate = jax.nn.sigmoid(jnp.dot(_rms(h, gple_ref[...]).astype(BF16), wg_ref[...],
                                  preferred_element_type=F32))
    proj = jnp.dot(p_ref[...].astype(BF16), wp_ref[...], preferred_element_type=F32)
    h = h + gate * proj
    o_ref[...] = _rms(h, gfin_ref[...]) if last else h


def _ple_final(h, p, g_ple, w_gate, w_proj, g_final, tile, last):
    n = h.shape[0]
    full = lambda shape: pl.BlockSpec(shape, lambda i: (0,) * len(shape))
    return pl.pallas_call(
        functools.partial(_ple_final_kernel, last=last),
        grid=(n // tile,),
        in_specs=[pl.BlockSpec((tile, D_MODEL), lambda i: (i, 0)),
                  pl.BlockSpec((tile, PLE_DIM), lambda i: (i, 0)),
                  full((1, D_MODEL)), full((D_MODEL, D_MODEL)), full((PLE_DIM, D_MODEL)),
                  full((1, D_MODEL))],
        out_specs=pl.BlockSpec((tile, D_MODEL), lambda i: (i, 0)),
        out_shape=jax.ShapeDtypeStruct((n, D_MODEL), F32),
        compiler_params=_params(("parallel",)),
        name="ple_final",
    )(h, p, g_ple, w_gate, w_proj, g_final)


def _rope_tables(positions):
    half = ROPE_DIM // 2
    dim = jnp.arange(LANES) % HEAD_DIM
    inv_freq = jnp.where(dim < ROPE_DIM, ROPE_THETA ** (-2.0 * (dim % half).astype(F32) / ROPE_DIM), 0.0)
    ang = positions.astype(F32)[..., None] * inv_freq
    return jnp.cos(ang), jnp.sin(ang)


def kernel(x, p, positions, norm_mix, w_in, conv_w, conv_b, conv_ln_g, conv_ln_b, g_attn_out,
           g_conv_out, w_out, norm_ffn, peer_wq, sub_keys, expert_u, expert_v, norm_ple,
           w_ple_gate, w_ple_proj, final_norm):
    B, S, _ = x.shape
    depth = w_in.shape[0]
    n = B * S
    cos_t, sin_t = _rope_tables(positions)
    super_block = ATTN_BLOCK * max(d for _, d in DILATED_PATTERNS)
    vec = lambda t: t.reshape(1, -1)
    h = x
    for i in range(depth):
        q, k, v, glu = _mix_in(h, vec(norm_mix[i]), w_in[i].astype(BF16), cos_t, sin_t, tile=512)
        attn = _attention(q, k, v, sorted(DILATED_PATTERNS, key=lambda wd: -wd[1]), super_block)
        h, xnt = _mix_out(h, attn, glu, conv_w[i], vec(conv_b[i]), vec(conv_ln_g[i]),
                          vec(conv_ln_b[i]), vec(g_attn_out[i]), vec(g_conv_out[i]),
                          w_out[i].astype(BF16), vec(norm_ffn[i]), tile=256)
        w1, cnt, w2, r2 = _peer_select(xnt, peer_wq[i].T.astype(BF16), sub_keys[i].astype(BF16),
                                       tile=256)
        h = _peer_dense(xnt, expert_u[i].astype(BF16), expert_v[i].T.astype(BF16), w1, cnt, w2, r2,
                        h.reshape(n, D_MODEL), tile=512, chunk=1024)
        h = _ple_final(h, p[i].reshape(n, PLE_DIM), vec(norm_ple[i]), w_ple_gate[i].astype(BF16),
                       w_ple_proj[i].astype(BF16), vec(final_norm), tile=512, last=(i == depth - 1))
        h = h.reshape(B, S, D_MODEL)
    return h
```

```python
import functools
import math

import jax
import jax.numpy as jnp
from jax import lax
from jax.experimental import pallas as pl
from jax.experimental.pallas import tpu as pltpu

F32 = jnp.float32
BF16 = jnp.bfloat16

D_MODEL = 1024
PLE_DIM = 256
HEAD_DIM = 64
N_ATTN_HEADS = 8
D_ATTN = N_ATTN_HEADS * HEAD_DIM
D_CONV = D_MODEL // 2
D_IN = 3 * D_ATTN + 2 * D_CONV
CONV_WIDTH = 31
ROPE_DIM = HEAD_DIM // 4
ROPE_THETA = 500000.0
DILATED_PATTERNS = ((128, 1), (512, 4), (2048, 16))
ATTN_BLOCK = 128
N_KEYS = 128
N_EXPERTS = N_KEYS * N_KEYS
PEER_HEADS = 8
PEER_TOPK = 16
EPS = 1e-6

LANES = 128
SUB = 8
PAIR = 2 * LANES
HEADS_PER_LANE_TILE = LANES // HEAD_DIM
CONV_HALO = 32
VMEM_LIMIT = 56 * 1024 * 1024

NEG_INF = float("-inf")


def _params(sem):
    return pltpu.CompilerParams(dimension_semantics=sem, vmem_limit_bytes=VMEM_LIMIT)


def _rms(x, g):
    return x * lax.rsqrt(jnp.mean(x * x, axis=-1, keepdims=True) + EPS) * g


def _mix_in_kernel(x_ref, g_ref, w_ref, cos_ref, sin_ref, q_ref, k_ref, v_ref, glu_ref):
    xn = _rms(x_ref[...], g_ref[...])
    z = jnp.dot(xn.astype(BF16), w_ref[...], preferred_element_type=F32)
    half = ROPE_DIM // 2
    cos, sin = cos_ref[...], sin_ref[...]
    dim = lax.broadcasted_iota(jnp.int32, (1, LANES), 1) % HEAD_DIM
    sa = jnp.where(dim < half, -sin, 0.0)
    sb = jnp.where((dim >= half) & (dim < ROPE_DIM), sin, 0.0)

    def rope(tt):
        return tt * cos + pltpu.roll(tt, LANES - half, 1) * sa + pltpu.roll(tt, half, 1) * sb

    for i in range(D_ATTN // LANES):
        q_ref[i] = rope(z[:, i * LANES:(i + 1) * LANES])
        k_ref[i] = rope(z[:, D_ATTN + i * LANES:D_ATTN + (i + 1) * LANES])
        v_ref[i] = z[:, 2 * D_ATTN + i * LANES:2 * D_ATTN + (i + 1) * LANES]
    a = z[:, 3 * D_ATTN:3 * D_ATTN + D_CONV]
    gt = z[:, 3 * D_ATTN + D_CONV:]
    glu_ref[...] = a * jax.nn.sigmoid(gt)


def _mix_in(x, g, w_in, cos_t, sin_t, tile):
    B, S, _ = x.shape
    n_slab = D_ATTN // LANES
    row = lambda d: pl.BlockSpec((None, tile, d), lambda b, i: (b, i, 0))
    full = lambda shape: pl.BlockSpec(shape, lambda b, i: (0,) * len(shape))
    slab = pl.BlockSpec((None, n_slab, tile, LANES), lambda b, i: (b, 0, i, 0))
    return pl.pallas_call(
        _mix_in_kernel,
        grid=(B, S // tile),
        in_specs=[row(D_MODEL), full((1, D_MODEL)), full((D_MODEL, D_IN)), row(LANES), row(LANES)],
        out_specs=[slab, slab, slab, row(D_CONV)],
        out_shape=[jax.ShapeDtypeStruct((B, n_slab, S, LANES), F32)] * 3
        + [jax.ShapeDtypeStruct((B, S, D_CONV), F32)],
        compiler_params=_params(("parallel", "parallel")),
        name="mix_in",
    )(x, g, w_in, cos_t, sin_t)


COARSE_STRIDE = 4


def _attn_kernel(q_ref, kc_ref, vc_ref, kp_ref, vp_ref, o_ref, oacc_ref, lacc_ref, *stage, patterns):
    n = pl.program_id(2)
    blk = ATTN_BLOCK
    rows_blk = q_ref.shape[0]
    lane = lax.broadcasted_iota(jnp.int32, (1, LANES), 1)
    head0 = lane < HEAD_DIM
    scale = HEAD_DIM ** -0.5
    qscale = [jnp.where(head0, scale, 0.0).astype(BF16), jnp.where(head0, 0.0, scale).astype(BF16)]
    qi = lax.broadcasted_iota(jnp.int32, (blk, 2 * blk), 0)
    kj = lax.broadcasted_iota(jnp.int32, (blk, 2 * blk), 1)
    dist = blk + qi - kj
    seq_start = (kj >= blk) | (n > 0)

    for index, (window, dil) in enumerate(patterns):
        first = index == 0
        band = (dist >= 0) & (dist <= window // dil)
        band_first = band & seq_start
        coarse = COARSE_STRIDE if dil > COARSE_STRIDE else 1
        fine = dil // coarse
        prev_skip = rows_blk - blk * dil
        srcs = {"q": q_ref, "kc": kc_ref, "vc": vc_ref, "kp": kp_ref, "vp": vp_ref,
                "o": oacc_ref, "l": lacc_ref}
        if coarse > 1:
            assert first and dil % coarse == 0 and prev_skip % coarse == 0
            for name, dst in zip(("q", "kc", "vc", "kp", "vp"), stage[:5]):
                for p in range(coarse):
                    dst[p] = srcs[name][pl.ds(p, rows_blk // coarse, stride=coarse), :]
                srcs[name] = dst
            srcs["o"], srcs["l"] = stage[5], stage[6]

        def view(name, r, first_elem, count):
            skip = prev_skip if name in ("kp", "vp") else 0
            if coarse > 1:
                return srcs[name], (r % coarse, pl.ds(skip // coarse + r // coarse + fine * first_elem, count,
                                                       stride=fine), slice(None))
            if dil > 1:
                return srcs[name], (pl.ds(skip + r + dil * first_elem, count, stride=dil), slice(None))
            return srcs[name], (pl.ds(skip + first_elem, count), slice(None))

        def take(name, r, first_elem, count):
            ref, idx = view(name, r, first_elem, count)
            return ref[idx]

        def put(name, r, first_elem, value):
            ref, idx = view(name, r, first_elem, value.shape[0])
            ref[idx] = value

        for r in range(dil):
            for qb in range(rows_blk // (blk * dil)):
                mask = band_first if qb == 0 else band
                q2 = take("q", r, qb * blk, blk).astype(BF16)
                if qb == 0:
                    k2 = jnp.concatenate([take("kp", r, 0, blk), take("kc", r, 0, blk)], axis=0)
                    v2 = jnp.concatenate([take("vp", r, 0, blk), take("vc", r, 0, blk)], axis=0)
                else:
                    k2 = take("kc", r, (qb - 1) * blk, 2 * blk)
                    v2 = take("vc", r, (qb - 1) * blk, 2 * blk)
                k2, v2 = k2.astype(BF16), v2.astype(BF16)
                o_h, lse_h = [], []
                for hh in range(HEADS_PER_LANE_TILE):
                    s = lax.dot_general(q2 * qscale[hh], k2, (((1,), (1,)), ((), ())),
                                        preferred_element_type=F32)
                    s = jnp.where(mask, s, NEG_INF)
                    m = jnp.max(s, axis=-1, keepdims=True)
                    e = jnp.exp(s - m)
                    l = jnp.sum(e, axis=-1, keepdims=True)
                    o_h.append(jnp.dot(e.astype(BF16), v2, preferred_element_type=F32) / l)
                    lse_h.append(m + jnp.log(l))
                o_new = jnp.where(head0, o_h[0], o_h[1])
                lse_new = jnp.where(head0, lse_h[0], lse_h[1])
                if not first:
                    lse_old = take("l", r, qb * blk, blk)
                    mx = jnp.maximum(lse_old, lse_new)
                    w_old = jnp.exp(lse_old - mx)
                    w_new = jnp.exp(lse_new - mx)
                    tot = w_old + w_new
                    o_new = (take("o", r, qb * blk, blk) * w_old + o_new * w_new) / tot
                    lse_new = mx + jnp.log(tot)
                put("o", r, qb * blk, o_new)
                put("l", r, qb * blk, lse_new)

        if coarse > 1:
            for src, dst in ((stage[5], oacc_ref), (stage[6], lacc_ref)):
                for p in range(coarse):
                    dst[pl.ds(p, rows_blk // coarse, stride=coarse), :] = src[p]

    o_ref[...] = oacc_ref[...]


def _attention(q, k, v, patterns, super_block):
    B, n_slab, S, _ = q.shape
    assert S % super_block == 0 and all(super_block % (ATTN_BLOCK * d) == 0 for _, d in patterns)
    cur = pl.BlockSpec((None, None, super_block, LANES), lambda b, hp, n: (b, hp, n, 0))
    prev = pl.BlockSpec((None, None, super_block, LANES), lambda b, hp, n: (b, hp, jnp.maximum(n - 1, 0), 0))
    slab = pltpu.VMEM((super_block, LANES), F32)
    staged = pltpu.VMEM((COARSE_STRIDE, super_block // COARSE_STRIDE, LANES), F32)
    needs_stage = patterns[0][1] > COARSE_STRIDE
    return pl.pallas_call(
        functools.partial(_attn_kernel, patterns=tuple(patterns)),
        grid=(B, n_slab, S // super_block),
        in_specs=[cur, cur, cur, prev, prev],
        out_specs=cur,
        out_shape=jax.ShapeDtypeStruct(q.shape, F32),
        scratch_shapes=[slab, slab] + [staged] * (7 if needs_stage else 0),
        compiler_params=_params(("parallel", "parallel", "parallel")),
        name="attention",
    )(q, k, v, k, v)


def _mix_out_kernel(x_ref, attn_ref, glu_ref, halo_ref, cw_ref, cb_ref, lng_ref, lnb_ref,
                    ga_ref, gc_ref, wo_ref, gffn_ref, h_ref, xnt_ref, win_ref, rot_ref, *, tile):
    i = pl.program_id(1)
    halo = halo_ref[...]
    win_ref[0:CONV_HALO, :] = jnp.where(i > 0, halo, jnp.zeros_like(halo))
    win_ref[CONV_HALO:, :] = glu_ref[...]
    span = rot_ref.shape[1]
    for shift in range(1, SUB):
        rot_ref[shift - 1] = win_ref[pl.ds(shift, span), :]
    off = CONV_HALO - (CONV_WIDTH - 1)
    c = jnp.zeros((tile, D_CONV), F32)
    for j in range(CONV_WIDTH):
        shift, base = (off + j) % SUB, (off + j) // SUB * SUB
        rows = win_ref[base:base + tile, :] if shift == 0 else rot_ref[shift - 1, base:base + tile, :]
        c = c + rows * cw_ref[j:j + 1, :]
    c = c + cb_ref[...]
    mu = jnp.mean(c, axis=-1, keepdims=True)
    var = jnp.mean(jnp.square(c - mu), axis=-1, keepdims=True)
    c = (c - mu) * lax.rsqrt(var + EPS) * lng_ref[...] + lnb_ref[...]
    c = c * jax.nn.sigmoid(c)
    attn = jnp.concatenate([attn_ref[s] for s in range(D_ATTN // LANES)], axis=1)
    y = jnp.dot(_rms(attn, ga_ref[...]).astype(BF16), wo_ref[0:D_ATTN, :],
                preferred_element_type=F32)
    y = y + jnp.dot(_rms(c, gc_ref[...]).astype(BF16), wo_ref[D_ATTN:, :],
                    preferred_element_type=F32)
    h = x_ref[...] + y
    h_ref[...] = h
    xnt_ref[...] = _rms(h, gffn_ref[...]).T.astype(BF16)


def _mix_out(x, attn, glu, conv_w, conv_b, ln_g, ln_b, g_attn, g_conv, w_out, g_ffn, tile):
    B, S, _ = x.shape
    nt = S // tile
    hpt = tile // CONV_HALO
    row = lambda d: pl.BlockSpec((None, tile, d), lambda b, i: (b, i, 0))
    full = lambda shape: pl.BlockSpec(shape, lambda b, i: (0,) * len(shape))
    halo = pl.BlockSpec((None, CONV_HALO, D_CONV), lambda b, i: (b, jnp.maximum(i * hpt - 1, 0), 0))
    return pl.pallas_call(
        functools.partial(_mix_out_kernel, tile=tile),
        grid=(B, nt),
        in_specs=[row(D_MODEL),
                  pl.BlockSpec((None, D_ATTN // LANES, tile, LANES), lambda b, i: (b, 0, i, 0)),
                  row(D_CONV), halo,
                  full((CONV_WIDTH, D_CONV)), full((1, D_CONV)), full((1, D_CONV)), full((1, D_CONV)),
                  full((1, D_ATTN)), full((1, D_CONV)), full((D_MODEL, D_MODEL)), full((1, D_MODEL))],
        out_specs=[row(D_MODEL), pl.BlockSpec((D_MODEL, tile), lambda b, i: (0, b * nt + i))],
        out_shape=[jax.ShapeDtypeStruct((B, S, D_MODEL), F32),
                   jax.ShapeDtypeStruct((D_MODEL, B * S), BF16)],
        scratch_shapes=[pltpu.VMEM((CONV_HALO + tile, D_CONV), F32),
                        pltpu.VMEM((SUB - 1, CONV_HALO + tile - SUB, D_CONV), F32)],
        compiler_params=_params(("parallel", "parallel")),
        name="mix_out",
    )(x, attn, glu, glu, conv_w, conv_b, ln_g, ln_b, g_attn, g_conv, w_out, g_ffn)


def _sorting_network(n):
    pairs, p = [], 1
    while p < n:
        k = p
        while k >= 1:
            for j in range(k % p, n - k, 2 * k):
                for i in range(min(k, n - j - k)):
                    if (i + j) // (2 * p) == (i + j + k) // (2 * p):
                        pairs.append((i + j, i + j + k))
            k //= 2
        p *= 2
    return pairs


def _sort_desc(v, pairs):
    v = list(v)
    for i, j in pairs:
        v[i], v[j] = jnp.maximum(v[i], v[j]), jnp.minimum(v[i], v[j])
    return v


def _bitonic_sort_desc(v):
    v = list(v)
    d = len(v) // 2
    while d >= 1:
        for i in range(len(v)):
            if i & d == 0:
                v[i], v[i + d] = jnp.maximum(v[i], v[i + d]), jnp.minimum(v[i], v[i + d])
        d //= 2
    return v


def _merge_sublanes(v, shift):
    n = len(v)
    return _bitonic_sort_desc([jnp.maximum(v[k], pltpu.roll(v[n - 1 - k], shift, 0)) for k in range(n)])


def _top16_over_rows(tiles):
    v = _sort_desc(tiles, _sorting_network(len(tiles)))
    for shift in (4, 2, 1):
        v = _merge_sublanes(v, shift)
    return v


def _peer_select_kernel(xt_ref, wq_ref, keys_ref, w1_ref, c_ref, w2_ref, r2_ref, q_ref):
    lt = pl.program_id(1)
    n_lt = q_ref.shape[0]

    @pl.when(lt == 0)
    def _():
        q = jnp.dot(wq_ref[...], xt_ref[...], preferred_element_type=F32).astype(BF16)
        for k in range(n_lt):
            q_ref[k] = q[:, k * LANES:(k + 1) * LANES]

    sub = lax.broadcasted_iota(jnp.int32, (SUB, LANES), 0)
    groups = N_KEYS // SUB

    def compact(v, lo):
        out = v[lo + SUB - 1]
        for s in range(SUB - 2, -1, -1):
            out = jnp.where(sub == s, v[lo + s], out)
        return out

    def head(h, carry):
        base = pl.multiple_of(h * 2 * N_KEYS, 2 * N_KEYS)
        s1 = jnp.dot(keys_ref[0], q_ref[lt, pl.ds(base, N_KEYS), :], preferred_element_type=F32)
        s2 = jnp.dot(keys_ref[1], q_ref[lt, pl.ds(base + N_KEYS, N_KEYS), :], preferred_element_type=F32)
        s1 = [s1[g * SUB:(g + 1) * SUB, :] for g in range(groups)]
        s2 = [s2[g * SUB:(g + 1) * SUB, :] for g in range(groups)]
        a = _top16_over_rows(s1)
        b = _top16_over_rows(s2)
        a_lo, a_hi, b_lo, b_hi = compact(a, 0), compact(a, SUB), compact(b, 0), compact(b, SUB)
        neg = jnp.full((SUB, LANES), NEG_INF, F32)
        c_j2 = jnp.where(sub < 5, a[2] + b_lo, neg)
        c_j3 = jnp.where(sub < 4, a[3] + b_lo, neg)
        c_j4 = jnp.where(sub < 3, a[4] + b_lo, neg)
        c_k0 = jnp.where(sub >= 5, a_lo + b[0], neg)
        c_k1 = jnp.where(sub >= 5, a_lo + b[1], neg)
        cands = [a[0] + b_lo, a[0] + b_hi, a[1] + b_lo,
                 jnp.where(sub < 5, c_j2, pltpu.roll(c_j4, 5, 0)),
                 jnp.where(sub < 4, c_j3, c_k0), c_k1, a_hi + b[0], neg]
        v = _sort_desc(cands, _sorting_network(len(cands)))
        v = _bitonic_sort_desc(v + [pltpu.roll(t, 4, 0) for t in reversed(v)])
        for shift in (2, 1):
            v = _merge_sublanes(v, shift)
        tau = v[PEER_TOPK - 1]
        top = a[0] + b[0]
        zsum = None
        for c in cands[:-1]:
            term = jnp.where(c >= tau, jnp.exp(c - top), 0.0)
            zsum = term if zsum is None else zsum + term
        for shift in (4, 2, 1):
            zsum = zsum + pltpu.roll(zsum, shift, 0)
        scale = 0.5 / zsum
        extra = []
        for j, kmax in ((0, 15), (1, 7), (2, 4)):
            e = jnp.zeros((SUB, LANES), F32)
            for k in range(4, kmax + 1):
                e = jnp.where(a[j] + b[k] >= tau, float(k - 3), e)
            extra.append(e)
        cnt, rank, w1, w2 = [], [], [], []
        for g in range(groups):
            c = jnp.zeros((SUB, LANES), F32)
            for k in range(4):
                c = jnp.where(s1[g] + b[k] >= tau, float(k + 1), c)
            c = c + jnp.where(s1[g] == a[0], extra[0],
                              jnp.where(s1[g] == a[1], extra[1], jnp.where(s1[g] == a[2], extra[2], 0.0)))
            r = jnp.zeros((SUB, LANES), F32)
            for k in range(PEER_TOPK):
                r = jnp.where(b[k] > s2[g], float(k + 1), r)
            cnt.append(c)
            rank.append(r)
            w1.append(jnp.exp(s1[g] - a[0]) * scale)
            w2.append(jnp.exp(s2[g] - b[0]))
        w1_ref[h] = jnp.concatenate(w1, axis=0)
        c_ref[h] = jnp.concatenate(cnt, axis=0)
        w2_ref[h] = jnp.concatenate(w2, axis=0).astype(BF16)
        r2_ref[h] = jnp.concatenate(rank, axis=0).astype(BF16)
        return carry

    def head_pair(i, carry):
        head(2 * i, carry)
        return head(2 * i + 1, carry)

    lax.fori_loop(0, PEER_HEADS // 2, head_pair, 0)


def _peer_select(xnt, wq_t, keys, tile):
    n = xnt.shape[1]
    n_lt = tile // LANES
    tab = pl.BlockSpec((PEER_HEADS, N_KEYS, LANES), lambda i, lt: (0, 0, i * n_lt + lt))
    return pl.pallas_call(
        _peer_select_kernel,
        grid=(n // tile, n_lt),
        in_specs=[pl.BlockSpec((D_MODEL, tile), lambda i, lt: (0, i)),
                  pl.BlockSpec(wq_t.shape, lambda i, lt: (0, 0)),
                  pl.BlockSpec(keys.shape, lambda i, lt: (0, 0, 0))],
        out_specs=[tab, tab, tab, tab],
        out_shape=[jax.ShapeDtypeStruct((PEER_HEADS, N_KEYS, n), F32)] * 2
        + [jax.ShapeDtypeStruct((PEER_HEADS, N_KEYS, n), BF16)] * 2,
        scratch_shapes=[pltpu.VMEM((n_lt, PEER_HEADS * 2 * N_KEYS, LANES), BF16)],
        compiler_params=_params(("parallel", "arbitrary")),
        name="peer_select",
    )(xnt, wq_t, keys)


def _peer_dense_kernel(xt_ref, u_ref, vt_ref, w1_ref, c_ref, w2_ref, r2_ref, h_ref, o_ref,
                       hid_ref, act_ref, acc_ref):
    j = pl.program_id(1)
    tile = xt_ref.shape[1]
    rows_per_chunk = u_ref.shape[0] // N_KEYS

    @pl.when(j == 0)
    def _():
        acc_ref[...] = jnp.zeros_like(acc_ref)

    hid_ref[...] = jnp.dot(u_ref[...], xt_ref[...], preferred_element_type=F32)
    for row in range(rows_per_chunk):
        for lp in range(tile // PAIR):
            lanes = slice(lp * PAIR, (lp + 1) * PAIR)

            def bcast(ref, h):
                parts = [jnp.broadcast_to(ref[h, row:row + 1, lp * PAIR + k * LANES:lp * PAIR + (k + 1) * LANES],
                                          (SUB, LANES)) for k in range(PAIR // LANES)]
                return jnp.concatenate(parts, axis=1).astype(BF16)

            gate = [jnp.zeros((SUB, PAIR), BF16) for _ in range(N_KEYS // SUB)]
            for h in range(PEER_HEADS):
                cnt = bcast(c_ref, h)
                w1 = bcast(w1_ref, h)
                for v in range(N_KEYS // SUB):
                    keys = slice(v * SUB, (v + 1) * SUB)
                    sel = jnp.where(r2_ref[h, keys, lanes] < cnt, w2_ref[h, keys, lanes],
                                    jnp.zeros((SUB, PAIR), BF16))
                    gate[v] = gate[v] + sel * w1
            for v in range(N_KEYS // SUB):
                rows = slice(row * N_KEYS + v * SUB, row * N_KEYS + (v + 1) * SUB)
                hid = hid_ref[rows, lanes]
                gelu2 = hid + hid * lax.erf(hid * math.sqrt(0.5))
                act_ref[rows, lanes] = gelu2.astype(BF16) * gate[v]
    acc_ref[...] += jnp.dot(vt_ref[...], act_ref[...], preferred_element_type=F32)

    @pl.when(j == pl.num_programs(1) - 1)
    def _():
        o_ref[...] = h_ref[...] + acc_ref[...].T


def _peer_dense(xnt, u, vt, w1, cnt, w2, r2, h, tile, chunk):
    n = xnt.shape[1]
    rows_per_chunk = chunk // N_KEYS
    small = pl.BlockSpec((PEER_HEADS, rows_per_chunk, tile), lambda i, j: (0, j, i))
    big = pl.BlockSpec((PEER_HEADS, N_KEYS, tile), lambda i, j: (0, 0, i))
    res = pl.BlockSpec((tile, D_MODEL), lambda i, j: (i, 0))
    return pl.pallas_call(
        _peer_dense_kernel,
        grid=(n // tile, N_EXPERTS // chunk),
        in_specs=[pl.BlockSpec((D_MODEL, tile), lambda i, j: (0, i)),
                  pl.BlockSpec((chunk, D_MODEL), lambda i, j: (j, 0)),
                  pl.BlockSpec((D_MODEL, chunk), lambda i, j: (0, j)),
                  small, small, big, big, res],
        out_specs=res,
        out_shape=jax.ShapeDtypeStruct((n, D_MODEL), F32),
        scratch_shapes=[pltpu.VMEM((chunk, tile), F32), pltpu.VMEM((chunk, tile), BF16),
                        pltpu.VMEM((D_MODEL, tile), F32)],
        compiler_params=_params(("parallel", "arbitrary")),
        name="peer_dense",
    )(xnt, u, vt, w1, cnt, w2, r2, h)


def _ple_final_kernel(h_ref, p_ref, gple_ref, wg_ref, wp_ref, gfin_ref, o_ref, *, last):
    h = h_ref[...]
    gate = jax.nn.sigmoid(jnp.dot(_rms(h, gple_ref[...]).astype(BF16), wg_ref[...],
                                  preferred_element_type=F32))
    proj = jnp.dot(p_ref[...].astype(BF16), wp_ref[...], preferred_element_type=F32)
    h = h + gate * proj
    o_ref[...] = _rms(h, gfin_ref[...]) if last else h


def _ple_final(h, p, g_ple, w_gate, w_proj, g_final, tile, last):
    n = h.shape[0]
    full = lambda shape: pl.BlockSpec(shape, lambda i: (0,) * len(shape))
    return pl.pallas_call(
        functools.partial(_ple_final_kernel, last=last),
        grid=(n // tile,),
        in_specs=[pl.BlockSpec((tile, D_MODEL), lambda i: (i, 0)),
                  pl.BlockSpec((tile, PLE_DIM), lambda i: (i, 0)),
                  full((1, D_MODEL)), full((D_MODEL, D_MODEL)), full((PLE_DIM, D_MODEL)),
                  full((1, D_MODEL))],
        out_specs=pl.BlockSpec((tile, D_MODEL), lambda i: (i, 0)),
        out_shape=jax.ShapeDtypeStruct((n, D_MODEL), F32),
        compiler_params=_params(("parallel",)),
        name="ple_final",
    )(h, p, g_ple, w_gate, w_proj, g_final)


def _rope_tables(positions):
    half = ROPE_DIM // 2
    dim = jnp.arange(LANES) % HEAD_DIM
    inv_freq = jnp.where(dim < ROPE_DIM, ROPE_THETA ** (-2.0 * (dim % half).astype(F32) / ROPE_DIM), 0.0)
    ang = positions.astype(F32)[..., None] * inv_freq
    return jnp.cos(ang), jnp.sin(ang)


def kernel(x, p, positions, norm_mix, w_in, conv_w, conv_b, conv_ln_g, conv_ln_b, g_attn_out,
           g_conv_out, w_out, norm_ffn, peer_wq, sub_keys, expert_u, expert_v, norm_ple,
           w_ple_gate, w_ple_proj, final_norm):
    B, S, _ = x.shape
    depth = w_in.shape[0]
    n = B * S
    cos_t, sin_t = _rope_tables(positions)
    super_block = ATTN_BLOCK * max(d for _, d in DILATED_PATTERNS)
    vec = lambda t: t.reshape(1, -1)
    h = x
    for i in range(depth):
        q, k, v, glu = _mix_in(h, vec(norm_mix[i]), w_in[i].astype(BF16), cos_t, sin_t, tile=512)
        attn = _attention(q, k, v, sorted(DILATED_PATTERNS, key=lambda wd: -wd[1]), super_block)
        h, xnt = _mix_out(h, attn, glu, conv_w[i], vec(conv_b[i]), vec(conv_ln_g[i]),
                          vec(conv_ln_b[i]), vec(g_attn_out[i]), vec(g_conv_out[i]),
                          w_out[i].astype(BF16), vec(norm_ffn[i]), tile=256)
        w1, cnt, w2, r2 = _peer_select(xnt, peer_wq[i].T.astype(BF16), sub_keys[i].astype(BF16),
                                       tile=256)
        h = _peer_dense(xnt, expert_u[i].astype(BF16), expert_v[i].T.astype(BF16), w1, cnt, w2, r2,
                        h.reshape(n, D_MODEL), tile=1024, chunk=1024)
        h = _ple_final(h, p[i].reshape(n, PLE_DIM), vec(norm_ple[i]), w_ple_gate[i].astype(BF16),
                       w_ple_proj[i].astype(BF16), vec(final_norm), tile=512, last=(i == depth - 1))
        h = h.reshape(B, S, D_MODEL)
    return h
```

```python
import functools
import math

import jax
import jax.numpy as jnp
from jax import lax
from jax.experimental import pallas as pl
from jax.experimental.pallas import tpu as pltpu

F32 = jnp.float32
BF16 = jnp.bfloat16

D_MODEL = 1024
PLE_DIM = 256
HEAD_DIM = 64
N_ATTN_HEADS = 8
D_ATTN = N_ATTN_HEADS * HEAD_DIM
D_CONV = D_MODEL // 2
D_IN = 3 * D_ATTN + 2 * D_CONV
CONV_WIDTH = 31
ROPE_DIM = HEAD_DIM // 4
ROPE_THETA = 500000.0
DILATED_PATTERNS = ((128, 1), (512, 4), (2048, 16))
ATTN_BLOCK = 128
N_KEYS = 128
N_EXPERTS = N_KEYS * N_KEYS
PEER_HEADS = 8
PEER_TOPK = 16
EPS = 1e-6

LANES = 128
SUB = 8
PAIR = 2 * LANES
HEADS_PER_LANE_TILE = LANES // HEAD_DIM
CONV_HALO = 32
VMEM_LIMIT = 56 * 1024 * 1024

NEG_INF = float("-inf")


def _params(sem):
    return pltpu.CompilerParams(dimension_semantics=sem, vmem_limit_bytes=VMEM_LIMIT)


def _rms(x, g):
    return x * lax.rsqrt(jnp.mean(x * x, axis=-1, keepdims=True) + EPS) * g


def _mix_in_kernel(x_ref, g_ref, w_ref, cos_ref, sin_ref, q_ref, k_ref, v_ref, glu_ref):
    xn = _rms(x_ref[...], g_ref[...])
    z = jnp.dot(xn.astype(BF16), w_ref[...], preferred_element_type=F32)
    half = ROPE_DIM // 2
    cos, sin = cos_ref[...], sin_ref[...]
    dim = lax.broadcasted_iota(jnp.int32, (1, LANES), 1) % HEAD_DIM
    sa = jnp.where(dim < half, -sin, 0.0)
    sb = jnp.where((dim >= half) & (dim < ROPE_DIM), sin, 0.0)

    def rope(tt):
        return tt * cos + pltpu.roll(tt, LANES - half, 1) * sa + pltpu.roll(tt, half, 1) * sb

    for i in range(D_ATTN // LANES):
        q_ref[i] = rope(z[:, i * LANES:(i + 1) * LANES])
        k_ref[i] = rope(z[:, D_ATTN + i * LANES:D_ATTN + (i + 1) * LANES])
        v_ref[i] = z[:, 2 * D_ATTN + i * LANES:2 * D_ATTN + (i + 1) * LANES]
    a = z[:, 3 * D_ATTN:3 * D_ATTN + D_CONV]
    gt = z[:, 3 * D_ATTN + D_CONV:]
    glu_ref[...] = a * jax.nn.sigmoid(gt)


def _mix_in(x, g, w_in, cos_t, sin_t, tile):
    B, S, _ = x.shape
    n_slab = D_ATTN // LANES
    row = lambda d: pl.BlockSpec((None, tile, d), lambda b, i: (b, i, 0))
    full = lambda shape: pl.BlockSpec(shape, lambda b, i: (0,) * len(shape))
    slab = pl.BlockSpec((None, n_slab, tile, LANES), lambda b, i: (b, 0, i, 0))
    return pl.pallas_call(
        _mix_in_kernel,
        grid=(B, S // tile),
        in_specs=[row(D_MODEL), full((1, D_MODEL)), full((D_MODEL, D_IN)), row(LANES), row(LANES)],
        out_specs=[slab, slab, slab, row(D_CONV)],
        out_shape=[jax.ShapeDtypeStruct((B, n_slab, S, LANES), F32)] * 3
        + [jax.ShapeDtypeStruct((B, S, D_CONV), F32)],
        compiler_params=_params(("parallel", "parallel")),
        name="mix_in",
    )(x, g, w_in, cos_t, sin_t)


COARSE_STRIDE = 4


def _attn_kernel(q_ref, kc_ref, vc_ref, kp_ref, vp_ref, o_ref, oacc_ref, lacc_ref, *stage, patterns):
    n = pl.program_id(2)
    blk = ATTN_BLOCK
    rows_blk = q_ref.shape[0]
    lane = lax.broadcasted_iota(jnp.int32, (1, LANES), 1)
    head0 = lane < HEAD_DIM
    scale = HEAD_DIM ** -0.5
    qscale = [jnp.where(head0, scale, 0.0).astype(BF16), jnp.where(head0, 0.0, scale).astype(BF16)]
    qi = lax.broadcasted_iota(jnp.int32, (blk, 2 * blk), 0)
    kj = lax.broadcasted_iota(jnp.int32, (blk, 2 * blk), 1)
    dist = blk + qi - kj
    seq_start = (kj >= blk) | (n > 0)

    for index, (window, dil) in enumerate(patterns):
        first = index == 0
        band = (dist >= 0) & (dist <= window // dil)
        band_first = band & seq_start
        coarse = COARSE_STRIDE if dil > COARSE_STRIDE else 1
        fine = dil // coarse
        prev_skip = rows_blk - blk * dil
        srcs = {"q": q_ref, "kc": kc_ref, "vc": vc_ref, "kp": kp_ref, "vp": vp_ref,
                "o": oacc_ref, "l": lacc_ref}
        if coarse > 1:
            assert first and dil % coarse == 0 and prev_skip % coarse == 0
            for name, dst in zip(("q", "kc", "vc", "kp", "vp"), stage[:5]):
                for p in range(coarse):
                    dst[p] = srcs[name][pl.ds(p, rows_blk // coarse, stride=coarse), :]
                srcs[name] = dst
            srcs["o"], srcs["l"] = stage[5], stage[6]

        def view(name, r, first_elem, count):
            skip = prev_skip if name in ("kp", "vp") else 0
            if coarse > 1:
                return srcs[name], (r % coarse, pl.ds(skip // coarse + r // coarse + fine * first_elem, count,
                                                       stride=fine), slice(None))
            if dil > 1:
                return srcs[name], (pl.ds(skip + r + dil * first_elem, count, stride=dil), slice(None))
            return srcs[name], (pl.ds(skip + first_elem, count), slice(None))

        def take(name, r, first_elem, count):
            ref, idx = view(name, r, first_elem, count)
            return ref[idx]

        def put(name, r, first_elem, value):
            ref, idx = view(name, r, first_elem, value.shape[0])
            ref[idx] = value

        for r in range(dil):
            for qb in range(rows_blk // (blk * dil)):
                mask = band_first if qb == 0 else band
                q2 = take("q", r, qb * blk, blk).astype(BF16)
                if qb == 0:
                    k2 = jnp.concatenate([take("kp", r, 0, blk), take("kc", r, 0, blk)], axis=0)
                    v2 = jnp.concatenate([take("vp", r, 0, blk), take("vc", r, 0, blk)], axis=0)
                else:
                    k2 = take("kc", r, (qb - 1) * blk, 2 * blk)
                    v2 = take("vc", r, (qb - 1) * blk, 2 * blk)
                k2, v2 = k2.astype(BF16), v2.astype(BF16)
                o_h, lse_h = [], []
                for hh in range(HEADS_PER_LANE_TILE):
                    s = lax.dot_general(q2 * qscale[hh], k2, (((1,), (1,)), ((), ())),
                                        preferred_element_type=F32)
                    s = jnp.where(mask, s, NEG_INF)
                    m = jnp.max(s, axis=-1, keepdims=True)
                    e = jnp.exp(s - m)
                    l = jnp.sum(e, axis=-1, keepdims=True)
                    o_h.append(jnp.dot(e.astype(BF16), v2, preferred_element_type=F32) / l)
                    lse_h.append(m + jnp.log(l))
                o_new = jnp.where(head0, o_h[0], o_h[1])
                lse_new = jnp.where(head0, lse_h[0], lse_h[1])
                if not first:
                    lse_old = take("l", r, qb * blk, blk)
                    mx = jnp.maximum(lse_old, lse_new)
                    w_old = jnp.exp(lse_old - mx)
                    w_new = jnp.exp(lse_new - mx)
                    tot = w_old + w_new
                    o_new = (take("o", r, qb * blk, blk) * w_old + o_new * w_new) / tot
                    lse_new = mx + jnp.log(tot)
                put("o", r, qb * blk, o_new)
                put("l", r, qb * blk, lse_new)

        if coarse > 1:
            for src, dst in ((stage[5], oacc_ref), (stage[6], lacc_ref)):
                for p in range(coarse):
                    dst[pl.ds(p, rows_blk // coarse, stride=coarse), :] = src[p]

    o_ref[...] = oacc_ref[...]


def _attention(q, k, v, patterns, super_block):
    B, n_slab, S, _ = q.shape
    assert S % super_block == 0 and all(super_block % (ATTN_BLOCK * d) == 0 for _, d in patterns)
    cur = pl.BlockSpec((None, None, super_block, LANES), lambda b, hp, n: (b, hp, n, 0))
    prev = pl.BlockSpec((None, None, super_block, LANES), lambda b, hp, n: (b, hp, jnp.maximum(n - 1, 0), 0))
    slab = pltpu.VMEM((super_block, LANES), F32)
    staged = pltpu.VMEM((COARSE_STRIDE, super_block // COARSE_STRIDE, LANES), F32)
    needs_stage = patterns[0][1] > COARSE_STRIDE
    return pl.pallas_call(
        functools.partial(_attn_kernel, patterns=tuple(patterns)),
        grid=(B, n_slab, S // super_block),
        in_specs=[cur, cur, cur, prev, prev],
        out_specs=cur,
        out_shape=jax.ShapeDtypeStruct(q.shape, F32),
        scratch_shapes=[slab, slab] + [staged] * (7 if needs_stage else 0),
        compiler_params=_params(("parallel", "parallel", "parallel")),
        name="attention",
    )(q, k, v, k, v)


def _mix_out_kernel(x_ref, attn_ref, glu_ref, halo_ref, cw_ref, cb_ref, lng_ref, lnb_ref,
                    ga_ref, gc_ref, wo_ref, gffn_ref, h_ref, xnt_ref, win_ref, rot_ref, *, tile):
    i = pl.program_id(1)
    halo = halo_ref[...]
    win_ref[0:CONV_HALO, :] = jnp.where(i > 0, halo, jnp.zeros_like(halo))
    win_ref[CONV_HALO:, :] = glu_ref[...]
    span = rot_ref.shape[1]
    for shift in range(1, SUB):
        rot_ref[shift - 1] = win_ref[pl.ds(shift, span), :]
    off = CONV_HALO - (CONV_WIDTH - 1)
    c = jnp.zeros((tile, D_CONV), F32)
    for j in range(CONV_WIDTH):
        shift, base = (off + j) % SUB, (off + j) // SUB * SUB
        rows = win_ref[base:base + tile, :] if shift == 0 else rot_ref[shift - 1, base:base + tile, :]
        c = c + rows * cw_ref[j:j + 1, :]
    c = c + cb_ref[...]
    mu = jnp.mean(c, axis=-1, keepdims=True)
    var = jnp.mean(jnp.square(c - mu), axis=-1, keepdims=True)
    c = (c - mu) * lax.rsqrt(var + EPS) * lng_ref[...] + lnb_ref[...]
    c = c * jax.nn.sigmoid(c)
    attn = jnp.concatenate([attn_ref[s] for s in range(D_ATTN // LANES)], axis=1)
    y = jnp.dot(_rms(attn, ga_ref[...]).astype(BF16), wo_ref[0:D_ATTN, :],
                preferred_element_type=F32)
    y = y + jnp.dot(_rms(c, gc_ref[...]).astype(BF16), wo_ref[D_ATTN:, :],
                    preferred_element_type=F32)
    h = x_ref[...] + y
    h_ref[...] = h
    xnt_ref[...] = _rms(h, gffn_ref[...]).T.astype(BF16)


def _mix_out(x, attn, glu, conv_w, conv_b, ln_g, ln_b, g_attn, g_conv, w_out, g_ffn, tile):
    B, S, _ = x.shape
    nt = S // tile
    hpt = tile // CONV_HALO
    row = lambda d: pl.BlockSpec((None, tile, d), lambda b, i: (b, i, 0))
    full = lambda shape: pl.BlockSpec(shape, lambda b, i: (0,) * len(shape))
    halo = pl.BlockSpec((None, CONV_HALO, D_CONV), lambda b, i: (b, jnp.maximum(i * hpt - 1, 0), 0))
    return pl.pallas_call(
        functools.partial(_mix_out_kernel, tile=tile),
        grid=(B, nt),
        in_specs=[row(D_MODEL),
                  pl.BlockSpec((None, D_ATTN // LANES, tile, LANES), lambda b, i: (b, 0, i, 0)),
                  row(D_CONV), halo,
                  full((CONV_WIDTH, D_CONV)), full((1, D_CONV)), full((1, D_CONV)), full((1, D_CONV)),
                  full((1, D_ATTN)), full((1, D_CONV)), full((D_MODEL, D_MODEL)), full((1, D_MODEL))],
        out_specs=[row(D_MODEL), pl.BlockSpec((D_MODEL, tile), lambda b, i: (0, b * nt + i))],
        out_shape=[jax.ShapeDtypeStruct((B, S, D_MODEL), F32),
                   jax.ShapeDtypeStruct((D_MODEL, B * S), BF16)],
        scratch_shapes=[pltpu.VMEM((CONV_HALO + tile, D_CONV), F32),
                        pltpu.VMEM((SUB - 1, CONV_HALO + tile - SUB, D_CONV), F32)],
        compiler_params=_params(("parallel", "parallel")),
        name="mix_out",
    )(x, attn, glu, glu, conv_w, conv_b, ln_g, ln_b, g_attn, g_conv, w_out, g_ffn)


def _sorting_network(n):
    pairs, p = [], 1
    while p < n:
        k = p
        while k >= 1:
            for j in range(k % p, n - k, 2 * k):
                for i in range(min(k, n - j - k)):
                    if (i + j) // (2 * p) == (i + j + k) // (2 * p):
                        pairs.append((i + j, i + j + k))
            k //= 2
        p *= 2
    return pairs


def _sort_desc(v, pairs):
    v = list(v)
    for i, j in pairs:
        v[i], v[j] = jnp.maximum(v[i], v[j]), jnp.minimum(v[i], v[j])
    return v


def _bitonic_sort_desc(v):
    v = list(v)
    d = len(v) // 2
    while d >= 1:
        for i in range(len(v)):
            if i & d == 0:
                v[i], v[i + d] = jnp.maximum(v[i], v[i + d]), jnp.minimum(v[i], v[i + d])
        d //= 2
    return v


def _merge_sublanes(v, shift):
    n = len(v)
    return _bitonic_sort_desc([jnp.maximum(v[k], pltpu.roll(v[n - 1 - k], shift, 0)) for k in range(n)])


def _top16_over_rows(tiles):
    v = _sort_desc(tiles, _sorting_network(len(tiles)))
    for shift in (4, 2, 1):
        v = _merge_sublanes(v, shift)
    return v


def _peer_select_kernel(xt_ref, wq_ref, keys_ref, w1_ref, c_ref, w2_ref, r2_ref, q_ref):
    n_lt = q_ref.shape[0]
    q = jnp.dot(wq_ref[...], xt_ref[...], preferred_element_type=F32).astype(BF16)
    for k in range(n_lt):
        q_ref[k] = q[:, k * LANES:(k + 1) * LANES]

    sub = lax.broadcasted_iota(jnp.int32, (SUB, LANES), 0)
    groups = N_KEYS // SUB

    def compact(v, lo):
        out = v[lo + SUB - 1]
        for s in range(SUB - 2, -1, -1):
            out = jnp.where(sub == s, v[lo + s], out)
        return out

    def head(h, lt):
        base = pl.multiple_of(h * 2 * N_KEYS, 2 * N_KEYS)
        s1 = jnp.dot(keys_ref[0], q_ref[lt, pl.ds(base, N_KEYS), :], preferred_element_type=F32)
        s2 = jnp.dot(keys_ref[1], q_ref[lt, pl.ds(base + N_KEYS, N_KEYS), :], preferred_element_type=F32)
        s1 = [s1[g * SUB:(g + 1) * SUB, :] for g in range(groups)]
        s2 = [s2[g * SUB:(g + 1) * SUB, :] for g in range(groups)]
        a = _top16_over_rows(s1)
        b = _top16_over_rows(s2)
        a_lo, a_hi, b_lo, b_hi = compact(a, 0), compact(a, SUB), compact(b, 0), compact(b, SUB)
        neg = jnp.full((SUB, LANES), NEG_INF, F32)
        c_j2 = jnp.where(sub < 5, a[2] + b_lo, neg)
        c_j3 = jnp.where(sub < 4, a[3] + b_lo, neg)
        c_j4 = jnp.where(sub < 3, a[4] + b_lo, neg)
        c_k0 = jnp.where(sub >= 5, a_lo + b[0], neg)
        c_k1 = jnp.where(sub >= 5, a_lo + b[1], neg)
        cands = [a[0] + b_lo, a[0] + b_hi, a[1] + b_lo,
                 jnp.where(sub < 5, c_j2, pltpu.roll(c_j4, 5, 0)),
                 jnp.where(sub < 4, c_j3, c_k0), c_k1, a_hi + b[0], neg]
        v = _sort_desc(cands, _sorting_network(len(cands)))
        v = _bitonic_sort_desc(v + [pltpu.roll(t, 4, 0) for t in reversed(v)])
        for shift in (2, 1):
            v = _merge_sublanes(v, shift)
        tau = v[PEER_TOPK - 1]
        top = a[0] + b[0]
        zsum = None
        for c in cands[:-1]:
            term = jnp.where(c >= tau, jnp.exp(c - top), 0.0)
            zsum = term if zsum is None else zsum + term
        for shift in (4, 2, 1):
            zsum = zsum + pltpu.roll(zsum, shift, 0)
        scale = 0.5 / zsum
        extra = []
        for j, kmax in ((0, 15), (1, 7), (2, 4)):
            e = jnp.zeros((SUB, LANES), F32)
            for k in range(4, kmax + 1):
                e = jnp.where(a[j] + b[k] >= tau, float(k - 3), e)
            extra.append(e)
        cnt, rank, w1, w2 = [], [], [], []
        for g in range(groups):
            c = jnp.zeros((SUB, LANES), F32)
            for k in range(4):
                c = jnp.where(s1[g] + b[k] >= tau, float(k + 1), c)
            c = c + jnp.where(s1[g] == a[0], extra[0],
                              jnp.where(s1[g] == a[1], extra[1], jnp.where(s1[g] == a[2], extra[2], 0.0)))
            r = jnp.zeros((SUB, LANES), F32)
            for k in range(PEER_TOPK):
                r = jnp.where(b[k] > s2[g], float(k + 1), r)
            cnt.append(c)
            rank.append(r)
            w1.append(jnp.exp(s1[g] - a[0]) * scale)
            w2.append(jnp.exp(s2[g] - b[0]))
        lanes = slice(lt * LANES, (lt + 1) * LANES)
        w1_ref[h, :, lanes] = jnp.concatenate(w1, axis=0)
        c_ref[h, :, lanes] = jnp.concatenate(cnt, axis=0)
        w2_ref[h, :, lanes] = jnp.concatenate(w2, axis=0).astype(BF16)
        r2_ref[h, :, lanes] = jnp.concatenate(rank, axis=0).astype(BF16)

    def head_pair(i, carry, lt):
        head(2 * i, lt)
        head(2 * i + 1, lt)
        return carry

    for lt in range(n_lt):
        lax.fori_loop(0, PEER_HEADS // 2, functools.partial(head_pair, lt=lt), 0)


def _peer_select(xnt, wq_t, keys, tile):
    n = xnt.shape[1]
    n_lt = tile // LANES
    tab = pl.BlockSpec((PEER_HEADS, N_KEYS, tile), lambda i: (0, 0, i))
    return pl.pallas_call(
        _peer_select_kernel,
        grid=(n // tile,),
        in_specs=[pl.BlockSpec((D_MODEL, tile), lambda i: (0, i)),
                  pl.BlockSpec(wq_t.shape, lambda i: (0, 0)),
                  pl.BlockSpec(keys.shape, lambda i: (0, 0, 0))],
        out_specs=[tab, tab, tab, tab],
        out_shape=[jax.ShapeDtypeStruct((PEER_HEADS, N_KEYS, n), F32)] * 2
        + [jax.ShapeDtypeStruct((PEER_HEADS, N_KEYS, n), BF16)] * 2,
        scratch_shapes=[pltpu.VMEM((n_lt, PEER_HEADS * 2 * N_KEYS, LANES), BF16)],
        compiler_params=_params(("parallel",)),
        name="peer_select",
    )(xnt, wq_t, keys)


def _peer_dense_kernel(xt_ref, u_ref, vt_ref, w1_ref, c_ref, w2_ref, r2_ref, h_ref, o_ref,
                       hid_ref, act_ref, acc_ref):
    j = pl.program_id(1)
    tile = xt_ref.shape[1]
    rows_per_chunk = u_ref.shape[0] // N_KEYS

    @pl.when(j == 0)
    def _():
        acc_ref[...] = jnp.zeros_like(acc_ref)

    hid_ref[...] = jnp.dot(u_ref[...], xt_ref[...], preferred_element_type=F32)
    for row in range(rows_per_chunk):
        for lp in range(tile // PAIR):
            lanes = slice(lp * PAIR, (lp + 1) * PAIR)

            def bcast(ref, h):
                parts = [jnp.broadcast_to(ref[h, row:row + 1, lp * PAIR + k * LANES:lp * PAIR + (k + 1) * LANES],
                                          (SUB, LANES)) for k in range(PAIR // LANES)]
                return jnp.concatenate(parts, axis=1).astype(BF16)

            gate = [jnp.zeros((SUB, PAIR), BF16) for _ in range(N_KEYS // SUB)]
            for h in range(PEER_HEADS):
                cnt = bcast(c_ref, h)
                w1 = bcast(w1_ref, h)
                for v in range(N_KEYS // SUB):
                    keys = slice(v * SUB, (v + 1) * SUB)
                    sel = jnp.where(r2_ref[h, keys, lanes] < cnt, w2_ref[h, keys, lanes],
                                    jnp.zeros((SUB, PAIR), BF16))
                    gate[v] = gate[v] + sel * w1
            for v in range(N_KEYS // SUB):
                rows = slice(row * N_KEYS + v * SUB, row * N_KEYS + (v + 1) * SUB)
                hid = hid_ref[rows, lanes]
                gelu2 = hid + hid * lax.erf(hid * math.sqrt(0.5))
                act_ref[rows, lanes] = gelu2.astype(BF16) * gate[v]
    acc_ref[...] += jnp.dot(vt_ref[...], act_ref[...], preferred_element_type=F32)

    @pl.when(j == pl.num_programs(1) - 1)
    def _():
        o_ref[...] = h_ref[...] + acc_ref[...].T


def _peer_dense(xnt, u, vt, w1, cnt, w2, r2, h, tile, chunk):
    n = xnt.shape[1]
    rows_per_chunk = chunk // N_KEYS
    small = pl.BlockSpec((PEER_HEADS, rows_per_chunk, tile), lambda i, j: (0, j, i))
    big = pl.BlockSpec((PEER_HEADS, N_KEYS, tile), lambda i, j: (0, 0, i))
    res = pl.BlockSpec((tile, D_MODEL), lambda i, j: (i, 0))
    return pl.pallas_call(
        _peer_dense_kernel,
        grid=(n // tile, N_EXPERTS // chunk),
        in_specs=[pl.BlockSpec((D_MODEL, tile), lambda i, j: (0, i)),
                  pl.BlockSpec((chunk, D_MODEL), lambda i, j: (j, 0)),
                  pl.BlockSpec((D_MODEL, chunk), lambda i, j: (0, j)),
                  small, small, big, big, res],
        out_specs=res,
        out_shape=jax.ShapeDtypeStruct((n, D_MODEL), F32),
        scratch_shapes=[pltpu.VMEM((chunk, tile), F32), pltpu.VMEM((chunk, tile), BF16),
                        pltpu.VMEM((D_MODEL, tile), F32)],
        compiler_params=_params(("parallel", "arbitrary")),
        name="peer_dense",
    )(xnt, u, vt, w1, cnt, w2, r2, h)


def _ple_final_kernel(h_ref, p_ref, gple_ref, wg_ref, wp_ref, gfin_ref, o_ref, *, last):
    h = h_ref[...]
    gate = jax.nn.sigmoid(jnp.dot(_rms(h, gple_ref[...]).astype(BF16), wg_ref[...],
                                  preferred_element_type=F32))
    proj = jnp.dot(p_ref[...].astype(BF16), wp_ref[...], preferred_element_type=F32)
    h = h + gate * proj
    o_ref[...] = _rms(h, gfin_ref[...]) if last else h


def _ple_final(h, p, g_ple, w_gate, w_proj, g_final, tile, last):
    n = h.shape[0]
    full = lambda shape: pl.BlockSpec(shape, lambda i: (0,) * len(shape))
    return pl.pallas_call(
        functools.partial(_ple_final_kernel, last=last),
        grid=(n // tile,),
        in_specs=[pl.BlockSpec((tile, D_MODEL), lambda i: (i, 0)),
                  pl.BlockSpec((tile, PLE_DIM), lambda i: (i, 0)),
                  full((1, D_MODEL)), full((D_MODEL, D_MODEL)), full((PLE_DIM, D_MODEL)),
                  full((1, D_MODEL))],
        out_specs=pl.BlockSpec((tile, D_MODEL), lambda i: (i, 0)),
        out_shape=jax.ShapeDtypeStruct((n, D_MODEL), F32),
        compiler_params=_params(("parallel",)),
        name="ple_final",
    )(h, p, g_ple, w_gate, w_proj, g_final)


def _rope_tables(positions):
    half = ROPE_DIM // 2
    dim = jnp.arange(LANES) % HEAD_DIM
    inv_freq = jnp.where(dim < ROPE_DIM, ROPE_THETA ** (-2.0 * (dim % half).astype(F32) / ROPE_DIM), 0.0)
    ang = positions.astype(F32)[..., None] * inv_freq
    return jnp.cos(ang), jnp.sin(ang)


def kernel(x, p, positions, norm_mix, w_in, conv_w, conv_b, conv_ln_g, conv_ln_b, g_attn_out,
           g_conv_out, w_out, norm_ffn, peer_wq, sub_keys, expert_u, expert_v, norm_ple,
           w_ple_gate, w_ple_proj, final_norm):
    B, S, _ = x.shape
    depth = w_in.shape[0]
    n = B * S
    cos_t, sin_t = _rope_tables(positions)
    super_block = ATTN_BLOCK * max(d for _, d in DILATED_PATTERNS)
    vec = lambda t: t.reshape(1, -1)
    h = x
    for i in range(depth):
        q, k, v, glu = _mix_in(h, vec(norm_mix[i]), w_in[i].astype(BF16), cos_t, sin_t, tile=1024)
        attn = _attention(q, k, v, sorted(DILATED_PATTERNS, key=lambda wd: -wd[1]), super_block)
        h, xnt = _mix_out(h, attn, glu, conv_w[i], vec(conv_b[i]), vec(conv_ln_g[i]),
                          vec(conv_ln_b[i]), vec(g_attn_out[i]), vec(g_conv_out[i]),
                          w_out[i].astype(BF16), vec(norm_ffn[i]), tile=512)
        w1, cnt, w2, r2 = _peer_select(xnt, peer_wq[i].T.astype(BF16), sub_keys[i].astype(BF16),
                                       tile=512)
        h = _peer_dense(xnt, expert_u[i].astype(BF16), expert_v[i].T.astype(BF16), w1, cnt, w2, r2,
                        h.reshape(n, D_MODEL), tile=1024, chunk=1024)
        h = _ple_final(h, p[i].reshape(n, PLE_DIM), vec(norm_ple[i]), w_ple_gate[i].astype(BF16),
                       w_ple_proj[i].astype(BF16), vec(final_norm), tile=1024, last=(i == depth - 1))
        h = h.reshape(B, S, D_MODEL)
    return h
```

```python
import functools
import math

import jax
import jax.numpy as jnp
from jax import lax
from jax.experimental import pallas as pl
from jax.experimental.pallas import tpu as pltpu

F32 = jnp.float32
BF16 = jnp.bfloat16

D_MODEL = 1024
PLE_DIM = 256
HEAD_DIM = 64
N_ATTN_HEADS = 8
D_ATTN = N_ATTN_HEADS * HEAD_DIM
D_CONV = D_MODEL // 2
D_IN = 3 * D_ATTN + 2 * D_CONV
CONV_WIDTH = 31
ROPE_DIM = HEAD_DIM // 4
ROPE_THETA = 500000.0
DILATED_PATTERNS = ((128, 1), (512, 4), (2048, 16))
ATTN_BLOCK = 128
N_KEYS = 128
N_EXPERTS = N_KEYS * N_KEYS
PEER_HEADS = 8
PEER_TOPK = 16
EPS = 1e-6

LANES = 128
SUB = 8
PAIR = 2 * LANES
HEADS_PER_LANE_TILE = LANES // HEAD_DIM
CONV_HALO = 32
VMEM_LIMIT = 56 * 1024 * 1024
SELECT_HEADS_PER_TRIP = 8

NEG_INF = float("-inf")


def _params(sem):
    return pltpu.CompilerParams(dimension_semantics=sem, vmem_limit_bytes=VMEM_LIMIT)


def _rms(x, g):
    return x * lax.rsqrt(jnp.mean(x * x, axis=-1, keepdims=True) + EPS) * g


def _mix_in_kernel(x_ref, g_ref, w_ref, cos_ref, sin_ref, q_ref, k_ref, v_ref, glu_ref):
    xn = _rms(x_ref[...], g_ref[...])
    z = jnp.dot(xn.astype(BF16), w_ref[...], preferred_element_type=F32)
    half = ROPE_DIM // 2
    cos, sin = cos_ref[...], sin_ref[...]
    dim = lax.broadcasted_iota(jnp.int32, (1, LANES), 1) % HEAD_DIM
    sa = jnp.where(dim < half, -sin, 0.0)
    sb = jnp.where((dim >= half) & (dim < ROPE_DIM), sin, 0.0)

    def rope(tt):
        return tt * cos + pltpu.roll(tt, LANES - half, 1) * sa + pltpu.roll(tt, half, 1) * sb

    for i in range(D_ATTN // LANES):
        q_ref[i] = rope(z[:, i * LANES:(i + 1) * LANES])
        k_ref[i] = rope(z[:, D_ATTN + i * LANES:D_ATTN + (i + 1) * LANES])
        v_ref[i] = z[:, 2 * D_ATTN + i * LANES:2 * D_ATTN + (i + 1) * LANES]
    a = z[:, 3 * D_ATTN:3 * D_ATTN + D_CONV]
    gt = z[:, 3 * D_ATTN + D_CONV:]
    glu_ref[...] = a * jax.nn.sigmoid(gt)


def _mix_in(x, g, w_in, cos_t, sin_t, tile):
    B, S, _ = x.shape
    n_slab = D_ATTN // LANES
    row = lambda d: pl.BlockSpec((None, tile, d), lambda b, i: (b, i, 0))
    full = lambda shape: pl.BlockSpec(shape, lambda b, i: (0,) * len(shape))
    slab = pl.BlockSpec((None, n_slab, tile, LANES), lambda b, i: (b, 0, i, 0))
    return pl.pallas_call(
        _mix_in_kernel,
        grid=(B, S // tile),
        in_specs=[row(D_MODEL), full((1, D_MODEL)), full((D_MODEL, D_IN)), row(LANES), row(LANES)],
        out_specs=[slab, slab, slab, row(D_CONV)],
        out_shape=[jax.ShapeDtypeStruct((B, n_slab, S, LANES), F32)] * 3
        + [jax.ShapeDtypeStruct((B, S, D_CONV), F32)],
        compiler_params=_params(("parallel", "parallel")),
        name="mix_in",
    )(x, g, w_in, cos_t, sin_t)


COARSE_STRIDE = 4


def _attn_kernel(q_ref, kc_ref, vc_ref, kp_ref, vp_ref, o_ref, oacc_ref, lacc_ref, *stage, patterns):
    n = pl.program_id(2)
    blk = ATTN_BLOCK
    rows_blk = q_ref.shape[0]
    lane = lax.broadcasted_iota(jnp.int32, (1, LANES), 1)
    head0 = lane < HEAD_DIM
    scale = HEAD_DIM ** -0.5
    qscale = [jnp.where(head0, scale, 0.0).astype(BF16), jnp.where(head0, 0.0, scale).astype(BF16)]
    qi = lax.broadcasted_iota(jnp.int32, (blk, 2 * blk), 0)
    kj = lax.broadcasted_iota(jnp.int32, (blk, 2 * blk), 1)
    dist = blk + qi - kj
    seq_start = (kj >= blk) | (n > 0)

    for index, (window, dil) in enumerate(patterns):
        first = index == 0
        band = (dist >= 0) & (dist <= window // dil)
        band_first = band & seq_start
        coarse = COARSE_STRIDE if dil > COARSE_STRIDE else 1
        fine = dil // coarse
        prev_skip = rows_blk - blk * dil
        srcs = {"q": q_ref, "kc": kc_ref, "vc": vc_ref, "kp": kp_ref, "vp": vp_ref,
                "o": oacc_ref, "l": lacc_ref}
        if coarse > 1:
            assert first and dil % coarse == 0 and prev_skip % coarse == 0
            for name, dst in zip(("q", "kc", "vc", "kp", "vp"), stage[:5]):
                for p in range(coarse):
                    dst[p] = srcs[name][pl.ds(p, rows_blk // coarse, stride=coarse), :]
                srcs[name] = dst
            srcs["o"], srcs["l"] = stage[5], stage[6]

        def view(name, r, first_elem, count):
            skip = prev_skip if name in ("kp", "vp") else 0
            if coarse > 1:
                return srcs[name], (r % coarse, pl.ds(skip // coarse + r // coarse + fine * first_elem, count,
                                                       stride=fine), slice(None))
            if dil > 1:
                return srcs[name], (pl.ds(skip + r + dil * first_elem, count, stride=dil), slice(None))
            return srcs[name], (pl.ds(skip + first_elem, count), slice(None))

        def take(name, r, first_elem, count):
            ref, idx = view(name, r, first_elem, count)
            return ref[idx]

        def put(name, r, first_elem, value):
            ref, idx = view(name, r, first_elem, value.shape[0])
            ref[idx] = value

        for r in range(dil):
            for qb in range(rows_blk // (blk * dil)):
                mask = band_first if qb == 0 else band
                q2 = take("q", r, qb * blk, blk).astype(BF16)
                if qb == 0:
                    k2 = jnp.concatenate([take("kp", r, 0, blk), take("kc", r, 0, blk)], axis=0)
                    v2 = jnp.concatenate([take("vp", r, 0, blk), take("vc", r, 0, blk)], axis=0)
                else:
                    k2 = take("kc", r, (qb - 1) * blk, 2 * blk)
                    v2 = take("vc", r, (qb - 1) * blk, 2 * blk)
                k2, v2 = k2.astype(BF16), v2.astype(BF16)
                o_h, lse_h = [], []
                for hh in range(HEADS_PER_LANE_TILE):
                    s = lax.dot_general(q2 * qscale[hh], k2, (((1,), (1,)), ((), ())),
                                        preferred_element_type=F32)
                    s = jnp.where(mask, s, NEG_INF)
                    m = jnp.max(s, axis=-1, keepdims=True)
                    e = jnp.exp(s - m)
                    l = jnp.sum(e, axis=-1, keepdims=True)
                    o_h.append(jnp.dot(e.astype(BF16), v2, preferred_element_type=F32) / l)
                    lse_h.append(m + jnp.log(l))
                o_new = jnp.where(head0, o_h[0], o_h[1])
                lse_new = jnp.where(head0, lse_h[0], lse_h[1])
                if not first:
                    lse_old = take("l", r, qb * blk, blk)
                    mx = jnp.maximum(lse_old, lse_new)
                    w_old = jnp.exp(lse_old - mx)
                    w_new = jnp.exp(lse_new - mx)
                    tot = w_old + w_new
                    o_new = (take("o", r, qb * blk, blk) * w_old + o_new * w_new) / tot
                    lse_new = mx + jnp.log(tot)
                put("o", r, qb * blk, o_new)
                put("l", r, qb * blk, lse_new)

        if coarse > 1:
            for src, dst in ((stage[5], oacc_ref), (stage[6], lacc_ref)):
                for p in range(coarse):
                    dst[pl.ds(p, rows_blk // coarse, stride=coarse), :] = src[p]

    o_ref[...] = oacc_ref[...]


def _attention(q, k, v, patterns, super_block):
    B, n_slab, S, _ = q.shape
    assert S % super_block == 0 and all(super_block % (ATTN_BLOCK * d) == 0 for _, d in patterns)
    cur = pl.BlockSpec((None, None, super_block, LANES), lambda b, hp, n: (b, hp, n, 0))
    prev = pl.BlockSpec((None, None, super_block, LANES), lambda b, hp, n: (b, hp, jnp.maximum(n - 1, 0), 0))
    slab = pltpu.VMEM((super_block, LANES), F32)
    staged = pltpu.VMEM((COARSE_STRIDE, super_block // COARSE_STRIDE, LANES), F32)
    needs_stage = patterns[0][1] > COARSE_STRIDE
    return pl.pallas_call(
        functools.partial(_attn_kernel, patterns=tuple(patterns)),
        grid=(B, n_slab, S // super_block),
        in_specs=[cur, cur, cur, prev, prev],
        out_specs=cur,
        out_shape=jax.ShapeDtypeStruct(q.shape, F32),
        scratch_shapes=[slab, slab] + [staged] * (7 if needs_stage else 0),
        compiler_params=_params(("parallel", "parallel", "parallel")),
        name="attention",
    )(q, k, v, k, v)


def _mix_out_kernel(x_ref, attn_ref, glu_ref, halo_ref, cw_ref, cb_ref, lng_ref, lnb_ref,
                    ga_ref, gc_ref, wo_ref, gffn_ref, h_ref, xnt_ref, win_ref, rot_ref, *, tile):
    i = pl.program_id(1)
    halo = halo_ref[...]
    win_ref[0:CONV_HALO, :] = jnp.where(i > 0, halo, jnp.zeros_like(halo))
    win_ref[CONV_HALO:, :] = glu_ref[...]
    span = rot_ref.shape[1]
    for shift in range(1, SUB):
        rot_ref[shift - 1] = win_ref[pl.ds(shift, span), :]
    off = CONV_HALO - (CONV_WIDTH - 1)
    c = jnp.zeros((tile, D_CONV), F32)
    for j in range(CONV_WIDTH):
        shift, base = (off + j) % SUB, (off + j) // SUB * SUB
        rows = win_ref[base:base + tile, :] if shift == 0 else rot_ref[shift - 1, base:base + tile, :]
        c = c + rows * cw_ref[j:j + 1, :]
    c = c + cb_ref[...]
    mu = jnp.mean(c, axis=-1, keepdims=True)
    var = jnp.mean(jnp.square(c - mu), axis=-1, keepdims=True)
    c = (c - mu) * lax.rsqrt(var + EPS) * lng_ref[...] + lnb_ref[...]
    c = c * jax.nn.sigmoid(c)
    attn = jnp.concatenate([attn_ref[s] for s in range(D_ATTN // LANES)], axis=1)
    y = jnp.dot(_rms(attn, ga_ref[...]).astype(BF16), wo_ref[0:D_ATTN, :],
                preferred_element_type=F32)
    y = y + jnp.dot(_rms(c, gc_ref[...]).astype(BF16), wo_ref[D_ATTN:, :],
                    preferred_element_type=F32)
    h = x_ref[...] + y
    h_ref[...] = h
    xnt_ref[...] = _rms(h, gffn_ref[...]).T.astype(BF16)


def _mix_out(x, attn, glu, conv_w, conv_b, ln_g, ln_b, g_attn, g_conv, w_out, g_ffn, tile):
    B, S, _ = x.shape
    nt = S // tile
    hpt = tile // CONV_HALO
    row = lambda d: pl.BlockSpec((None, tile, d), lambda b, i: (b, i, 0))
    full = lambda shape: pl.BlockSpec(shape, lambda b, i: (0,) * len(shape))
    halo = pl.BlockSpec((None, CONV_HALO, D_CONV), lambda b, i: (b, jnp.maximum(i * hpt - 1, 0), 0))
    return pl.pallas_call(
        functools.partial(_mix_out_kernel, tile=tile),
        grid=(B, nt),
        in_specs=[row(D_MODEL),
                  pl.BlockSpec((None, D_ATTN // LANES, tile, LANES), lambda b, i: (b, 0, i, 0)),
                  row(D_CONV), halo,
                  full((CONV_WIDTH, D_CONV)), full((1, D_CONV)), full((1, D_CONV)), full((1, D_CONV)),
                  full((1, D_ATTN)), full((1, D_CONV)), full((D_MODEL, D_MODEL)), full((1, D_MODEL))],
        out_specs=[row(D_MODEL), pl.BlockSpec((D_MODEL, tile), lambda b, i: (0, b * nt + i))],
        out_shape=[jax.ShapeDtypeStruct((B, S, D_MODEL), F32),
                   jax.ShapeDtypeStruct((D_MODEL, B * S), BF16)],
        scratch_shapes=[pltpu.VMEM((CONV_HALO + tile, D_CONV), F32),
                        pltpu.VMEM((SUB - 1, CONV_HALO + tile - SUB, D_CONV), F32)],
        compiler_params=_params(("parallel", "parallel")),
        name="mix_out",
    )(x, attn, glu, glu, conv_w, conv_b, ln_g, ln_b, g_attn, g_conv, w_out, g_ffn)


def _sorting_network(n):
    pairs, p = [], 1
    while p < n:
        k = p
        while k >= 1:
            for j in range(k % p, n - k, 2 * k):
                for i in range(min(k, n - j - k)):
                    if (i + j) // (2 * p) == (i + j + k) // (2 * p):
                        pairs.append((i + j, i + j + k))
            k //= 2
        p *= 2
    return pairs


def _sort_desc(v, pairs):
    v = list(v)
    for i, j in pairs:
        v[i], v[j] = jnp.maximum(v[i], v[j]), jnp.minimum(v[i], v[j])
    return v


def _bitonic_sort_desc(v):
    v = list(v)
    d = len(v) // 2
    while d >= 1:
        for i in range(len(v)):
            if i & d == 0:
                v[i], v[i + d] = jnp.maximum(v[i], v[i + d]), jnp.minimum(v[i], v[i + d])
        d //= 2
    return v


def _merge_sublanes(v, shift):
    n = len(v)
    return _bitonic_sort_desc([jnp.maximum(v[k], pltpu.roll(v[n - 1 - k], shift, 0)) for k in range(n)])


def _top16_over_rows(tiles):
    v = _sort_desc(tiles, _sorting_network(len(tiles)))
    for shift in (4, 2, 1):
        v = _merge_sublanes(v, shift)
    return v


def _peer_select_kernel(xt_ref, wq_ref, keys_ref, w1_ref, c_ref, w2_ref, r2_ref, q_ref):
    n_lt = q_ref.shape[0]
    q = jnp.dot(wq_ref[...], xt_ref[...], preferred_element_type=F32).astype(BF16)
    for k in range(n_lt):
        q_ref[k] = q[:, k * LANES:(k + 1) * LANES]

    sub = lax.broadcasted_iota(jnp.int32, (SUB, LANES), 0)
    groups = N_KEYS // SUB

    def compact(v, lo):
        out = v[lo + SUB - 1]
        for s in range(SUB - 2, -1, -1):
            out = jnp.where(sub == s, v[lo + s], out)
        return out

    def head(h, lt):
        base = pl.multiple_of(h * 2 * N_KEYS, 2 * N_KEYS)
        s1 = jnp.dot(keys_ref[0], q_ref[lt, pl.ds(base, N_KEYS), :], preferred_element_type=F32)
        s2 = jnp.dot(keys_ref[1], q_ref[lt, pl.ds(base + N_KEYS, N_KEYS), :], preferred_element_type=F32)
        s1 = [s1[g * SUB:(g + 1) * SUB, :] for g in range(groups)]
        s2 = [s2[g * SUB:(g + 1) * SUB, :] for g in range(groups)]
        a = _top16_over_rows(s1)
        b = _top16_over_rows(s2)
        a_lo, a_hi, b_lo, b_hi = compact(a, 0), compact(a, SUB), compact(b, 0), compact(b, SUB)
        neg = jnp.full((SUB, LANES), NEG_INF, F32)
        c_j2 = jnp.where(sub < 5, a[2] + b_lo, neg)
        c_j3 = jnp.where(sub < 4, a[3] + b_lo, neg)
        c_j4 = jnp.where(sub < 3, a[4] + b_lo, neg)
        c_k0 = jnp.where(sub >= 5, a_lo + b[0], neg)
        c_k1 = jnp.where(sub >= 5, a_lo + b[1], neg)
        cands = [a[0] + b_lo, a[0] + b_hi, a[1] + b_lo,
                 jnp.where(sub < 5, c_j2, pltpu.roll(c_j4, 5, 0)),
                 jnp.where(sub < 4, c_j3, c_k0), c_k1, a_hi + b[0], neg]
        v = _sort_desc(cands, _sorting_network(len(cands)))
        v = _bitonic_sort_desc(v + [pltpu.roll(t, 4, 0) for t in reversed(v)])
        for shift in (2, 1):
            v = _merge_sublanes(v, shift)
        tau = v[PEER_TOPK - 1]
        top = a[0] + b[0]
        zsum = None
        for c in cands[:-1]:
            term = jnp.where(c >= tau, jnp.exp(c - top), 0.0)
            zsum = term if zsum is None else zsum + term
        for shift in (4, 2, 1):
            zsum = zsum + pltpu.roll(zsum, shift, 0)
        scale = 0.5 / zsum
        extra = []
        for j, kmax in ((0, 15), (1, 7), (2, 4)):
            e = jnp.zeros((SUB, LANES), F32)
            for k in range(4, kmax + 1):
                e = jnp.where(a[j] + b[k] >= tau, float(k - 3), e)
            extra.append(e)
        cnt, rank, w1, w2 = [], [], [], []
        for g in range(groups):
            c = jnp.zeros((SUB, LANES), F32)
            for k in range(4):
                c = jnp.where(s1[g] + b[k] >= tau, float(k + 1), c)
            c = c + jnp.where(s1[g] == a[0], extra[0],
                              jnp.where(s1[g] == a[1], extra[1], jnp.where(s1[g] == a[2], extra[2], 0.0)))
            r = jnp.zeros((SUB, LANES), F32)
            for k in range(PEER_TOPK):
                r = jnp.where(b[k] > s2[g], float(k + 1), r)
            cnt.append(c)
            rank.append(r)
            w1.append(jnp.exp(s1[g] - a[0]) * scale)
            w2.append(jnp.exp(s2[g] - b[0]))
        lanes = slice(lt * LANES, (lt + 1) * LANES)
        w1_ref[h, :, lanes] = jnp.concatenate(w1, axis=0)
        c_ref[h, :, lanes] = jnp.concatenate(cnt, axis=0)
        w2_ref[h, :, lanes] = jnp.concatenate(w2, axis=0).astype(BF16)
        r2_ref[h, :, lanes] = jnp.concatenate(rank, axis=0).astype(BF16)

    def head_group(i, carry, lt):
        for u in range(SELECT_HEADS_PER_TRIP):
            head(SELECT_HEADS_PER_TRIP * i + u, lt)
        return carry

    for lt in range(n_lt):
        lax.fori_loop(0, PEER_HEADS // SELECT_HEADS_PER_TRIP, functools.partial(head_group, lt=lt), 0)


def _peer_select(xnt, wq_t, keys, tile):
    n = xnt.shape[1]
    n_lt = tile // LANES
    tab = pl.BlockSpec((PEER_HEADS, N_KEYS, tile), lambda i: (0, 0, i))
    return pl.pallas_call(
        _peer_select_kernel,
        grid=(n // tile,),
        in_specs=[pl.BlockSpec((D_MODEL, tile), lambda i: (0, i)),
                  pl.BlockSpec(wq_t.shape, lambda i: (0, 0)),
                  pl.BlockSpec(keys.shape, lambda i: (0, 0, 0))],
        out_specs=[tab, tab, tab, tab],
        out_shape=[jax.ShapeDtypeStruct((PEER_HEADS, N_KEYS, n), F32)] * 2
        + [jax.ShapeDtypeStruct((PEER_HEADS, N_KEYS, n), BF16)] * 2,
        scratch_shapes=[pltpu.VMEM((n_lt, PEER_HEADS * 2 * N_KEYS, LANES), BF16)],
        compiler_params=_params(("parallel",)),
        name="peer_select",
    )(xnt, wq_t, keys)


def _peer_dense_kernel(xt_ref, u_ref, vt_ref, w1_ref, c_ref, w2_ref, r2_ref, h_ref, o_ref,
                       hid_ref, act_ref, acc_ref):
    j = pl.program_id(1)
    tile = xt_ref.shape[1]
    rows_per_chunk = u_ref.shape[0] // N_KEYS

    @pl.when(j == 0)
    def _():
        acc_ref[...] = jnp.zeros_like(acc_ref)

    hid_ref[...] = jnp.dot(u_ref[...], xt_ref[...], preferred_element_type=F32)
    for row in range(rows_per_chunk):
        for lp in range(tile // PAIR):
            lanes = slice(lp * PAIR, (lp + 1) * PAIR)

            def bcast(ref, h):
                parts = [jnp.broadcast_to(ref[h, row:row + 1, lp * PAIR + k * LANES:lp * PAIR + (k + 1) * LANES],
                                          (SUB, LANES)) for k in range(PAIR // LANES)]
                return jnp.concatenate(parts, axis=1).astype(BF16)

            gate = [jnp.zeros((SUB, PAIR), BF16) for _ in range(N_KEYS // SUB)]
            for h in range(PEER_HEADS):
                cnt = bcast(c_ref, h)
                w1 = bcast(w1_ref, h)
                for v in range(N_KEYS // SUB):
                    keys = slice(v * SUB, (v + 1) * SUB)
                    sel = jnp.where(r2_ref[h, keys, lanes] < cnt, w2_ref[h, keys, lanes],
                                    jnp.zeros((SUB, PAIR), BF16))
                    gate[v] = gate[v] + sel * w1
            for v in range(N_KEYS // SUB):
                rows = slice(row * N_KEYS + v * SUB, row * N_KEYS + (v + 1) * SUB)
                hid = hid_ref[rows, lanes]
                gelu2 = hid + hid * lax.erf(hid * math.sqrt(0.5))
                act_ref[rows, lanes] = gelu2.astype(BF16) * gate[v]
    acc_ref[...] += jnp.dot(vt_ref[...], act_ref[...], preferred_element_type=F32)

    @pl.when(j == pl.num_programs(1) - 1)
    def _():
        o_ref[...] = h_ref[...] + acc_ref[...].T


def _peer_dense(xnt, u, vt, w1, cnt, w2, r2, h, tile, chunk):
    n = xnt.shape[1]
    rows_per_chunk = chunk // N_KEYS
    small = pl.BlockSpec((PEER_HEADS, rows_per_chunk, tile), lambda i, j: (0, j, i))
    big = pl.BlockSpec((PEER_HEADS, N_KEYS, tile), lambda i, j: (0, 0, i))
    res = pl.BlockSpec((tile, D_MODEL), lambda i, j: (i, 0))
    return pl.pallas_call(
        _peer_dense_kernel,
        grid=(n // tile, N_EXPERTS // chunk),
        in_specs=[pl.BlockSpec((D_MODEL, tile), lambda i, j: (0, i)),
                  pl.BlockSpec((chunk, D_MODEL), lambda i, j: (j, 0)),
                  pl.BlockSpec((D_MODEL, chunk), lambda i, j: (0, j)),
                  small, small, big, big, res],
        out_specs=res,
        out_shape=jax.ShapeDtypeStruct((n, D_MODEL), F32),
        scratch_shapes=[pltpu.VMEM((chunk, tile), F32), pltpu.VMEM((chunk, tile), BF16),
                        pltpu.VMEM((D_MODEL, tile), F32)],
        compiler_params=_params(("parallel", "arbitrary")),
        name="peer_dense",
    )(xnt, u, vt, w1, cnt, w2, r2, h)


def _ple_final_kernel(h_ref, p_ref, gple_ref, wg_ref, wp_ref, gfin_ref, o_ref, *, last):
    h = h_ref[...]
    gate = jax.nn.sigmoid(jnp.dot(_rms(h, gple_ref[...]).astype(BF16), wg_ref[...],
                                  preferred_element_type=F32))
    proj = jnp.dot(p_ref[...].astype(BF16), wp_ref[...], preferred_element_type=F32)
    h = h + gate * proj
    o_ref[...] = _rms(h, gfin_ref[...]) if last else h


def _ple_final(h, p, g_ple, w_gate, w_proj, g_final, tile, last):
    n = h.shape[0]
    full = lambda shape: pl.BlockSpec(shape, lambda i: (0,) * len(shape))
    return pl.pallas_call(
        functools.partial(_ple_final_kernel, last=last),
        grid=(n // tile,),
        in_specs=[pl.BlockSpec((tile, D_MODEL), lambda i: (i, 0)),
                  pl.BlockSpec((tile, PLE_DIM), lambda i: (i, 0)),
                  full((1, D_MODEL)), full((D_MODEL, D_MODEL)), full((PLE_DIM, D_MODEL)),
                  full((1, D_MODEL))],
        out_specs=pl.BlockSpec((tile, D_MODEL), lambda i: (i, 0)),
        out_shape=jax.ShapeDtypeStruct((n, D_MODEL), F32),
        compiler_params=_params(("parallel",)),
        name="ple_final",
    )(h, p, g_ple, w_gate, w_proj, g_final)


def _rope_tables(positions):
    half = ROPE_DIM // 2
    dim = jnp.arange(LANES) % HEAD_DIM
    inv_freq = jnp.where(dim < ROPE_DIM, ROPE_THETA ** (-2.0 * (dim % half).astype(F32) / ROPE_DIM), 0.0)
    ang = positions.astype(F32)[..., None] * inv_freq
    return jnp.cos(ang), jnp.sin(ang)


def kernel(x, p, positions, norm_mix, w_in, conv_w, conv_b, conv_ln_g, conv_ln_b, g_attn_out,
           g_conv_out, w_out, norm_ffn, peer_wq, sub_keys, expert_u, expert_v, norm_ple,
           w_ple_gate, w_ple_proj, final_norm):
    B, S, _ = x.shape
    depth = w_in.shape[0]
    n = B * S
    cos_t, sin_t = _rope_tables(positions)
    super_block = ATTN_BLOCK * max(d for _, d in DILATED_PATTERNS)
    vec = lambda t: t.reshape(1, -1)
    h = x
    for i in range(depth):
        q, k, v, glu = _mix_in(h, vec(norm_mix[i]), w_in[i].astype(BF16), cos_t, sin_t, tile=1024)
        attn = _attention(q, k, v, sorted(DILATED_PATTERNS, key=lambda wd: -wd[1]), super_block)
        h, xnt = _mix_out(h, attn, glu, conv_w[i], vec(conv_b[i]), vec(conv_ln_g[i]),
                          vec(conv_ln_b[i]), vec(g_attn_out[i]), vec(g_conv_out[i]),
                          w_out[i].astype(BF16), vec(norm_ffn[i]), tile=512)
        w1, cnt, w2, r2 = _peer_select(xnt, peer_wq[i].T.astype(BF16), sub_keys[i].astype(BF16),
                                       tile=512)
        h = _peer_dense(xnt, expert_u[i].astype(BF16), expert_v[i].T.astype(BF16), w1, cnt, w2, r2,
                        h.reshape(n, D_MODEL), tile=1024, chunk=1024)
        h = _ple_final(h, p[i].reshape(n, PLE_DIM), vec(norm_ple[i]), w_ple_gate[i].astype(BF16),
                       w_ple_proj[i].astype(BF16), vec(final_norm), tile=1024, last=(i == depth - 1))
        h = h.reshape(B, S, D_MODEL)
    return h
```

```python
import functools
import math

import jax
import jax.numpy as jnp
from jax import lax
from jax.experimental import pallas as pl
from jax.experimental.pallas import tpu as pltpu

F32 = jnp.float32
BF16 = jnp.bfloat16

D_MODEL = 1024
PLE_DIM = 256
HEAD_DIM = 64
N_ATTN_HEADS = 8
D_ATTN = N_ATTN_HEADS * HEAD_DIM
D_CONV = D_MODEL // 2
D_IN = 3 * D_ATTN + 2 * D_CONV
CONV_WIDTH = 31
ROPE_DIM = HEAD_DIM // 4
ROPE_THETA = 500000.0
DILATED_PATTERNS = ((128, 1), (512, 4), (2048, 16))
ATTN_BLOCK = 128
N_KEYS = 128
N_EXPERTS = N_KEYS * N_KEYS
PEER_HEADS = 8
PEER_TOPK = 16
EPS = 1e-6

LANES = 128
SUB = 8
PAIR = 2 * LANES
HEADS_PER_LANE_TILE = LANES // HEAD_DIM
CONV_HALO = 32
VMEM_LIMIT = 56 * 1024 * 1024
SELECT_HEADS_PER_TRIP = 8

NEG_INF = float("-inf")


def _params(sem):
    return pltpu.CompilerParams(dimension_semantics=sem, vmem_limit_bytes=VMEM_LIMIT)


def _rms(x, g):
    return x * lax.rsqrt(jnp.mean(x * x, axis=-1, keepdims=True) + EPS) * g


def _mix_in_kernel(x_ref, g_ref, w_ref, cos_ref, sin_ref, q_ref, k_ref, v_ref, glu_ref):
    xn = _rms(x_ref[...], g_ref[...])
    z = jnp.dot(xn.astype(BF16), w_ref[...], preferred_element_type=F32)
    half = ROPE_DIM // 2
    cos, sin = cos_ref[...], sin_ref[...]
    dim = lax.broadcasted_iota(jnp.int32, (1, LANES), 1) % HEAD_DIM
    sa = jnp.where(dim < half, -sin, 0.0)
    sb = jnp.where((dim >= half) & (dim < ROPE_DIM), sin, 0.0)

    def rope(tt):
        return tt * cos + pltpu.roll(tt, LANES - half, 1) * sa + pltpu.roll(tt, half, 1) * sb

    for i in range(D_ATTN // LANES):
        q_ref[i] = rope(z[:, i * LANES:(i + 1) * LANES])
        k_ref[i] = rope(z[:, D_ATTN + i * LANES:D_ATTN + (i + 1) * LANES])
        v_ref[i] = z[:, 2 * D_ATTN + i * LANES:2 * D_ATTN + (i + 1) * LANES]
    a = z[:, 3 * D_ATTN:3 * D_ATTN + D_CONV]
    gt = z[:, 3 * D_ATTN + D_CONV:]
    glu_ref[...] = a * jax.nn.sigmoid(gt)


def _mix_in(x, g, w_in, cos_t, sin_t, tile):
    B, S, _ = x.shape
    n_slab = D_ATTN // LANES
    row = lambda d: pl.BlockSpec((None, tile, d), lambda b, i: (b, i, 0))
    full = lambda shape: pl.BlockSpec(shape, lambda b, i: (0,) * len(shape))
    slab = pl.BlockSpec((None, n_slab, tile, LANES), lambda b, i: (b, 0, i, 0))
    return pl.pallas_call(
        _mix_in_kernel,
        grid=(B, S // tile),
        in_specs=[row(D_MODEL), full((1, D_MODEL)), full((D_MODEL, D_IN)), row(LANES), row(LANES)],
        out_specs=[slab, slab, slab, row(D_CONV)],
        out_shape=[jax.ShapeDtypeStruct((B, n_slab, S, LANES), F32)] * 3
        + [jax.ShapeDtypeStruct((B, S, D_CONV), F32)],
        compiler_params=_params(("parallel", "parallel")),
        name="mix_in",
    )(x, g, w_in, cos_t, sin_t)


COARSE_STRIDE = 4


def _attn_kernel(q_ref, kc_ref, vc_ref, kp_ref, vp_ref, o_ref, oacc_ref, lacc_ref, *stage, patterns):
    n = pl.program_id(2)
    blk = ATTN_BLOCK
    rows_blk = q_ref.shape[0]
    lane = lax.broadcasted_iota(jnp.int32, (1, LANES), 1)
    head0 = lane < HEAD_DIM
    scale = HEAD_DIM ** -0.5
    qscale = [jnp.where(head0, scale, 0.0).astype(BF16), jnp.where(head0, 0.0, scale).astype(BF16)]
    qi = lax.broadcasted_iota(jnp.int32, (blk, 2 * blk), 0)
    kj = lax.broadcasted_iota(jnp.int32, (blk, 2 * blk), 1)
    dist = blk + qi - kj
    seq_start = (kj >= blk) | (n > 0)

    for index, (window, dil) in enumerate(patterns):
        first = index == 0
        band = (dist >= 0) & (dist <= window // dil)
        band_first = band & seq_start
        coarse = COARSE_STRIDE if dil > COARSE_STRIDE else 1
        fine = dil // coarse
        prev_skip = rows_blk - blk * dil
        srcs = {"q": q_ref, "kc": kc_ref, "vc": vc_ref, "kp": kp_ref, "vp": vp_ref,
                "o": oacc_ref, "l": lacc_ref}
        if coarse > 1:
            assert first and dil % coarse == 0 and prev_skip % coarse == 0
            for name, dst in zip(("q", "kc", "vc", "kp", "vp"), stage[:5]):
                for p in range(coarse):
                    dst[p] = srcs[name][pl.ds(p, rows_blk // coarse, stride=coarse), :]
                srcs[name] = dst
            srcs["o"], srcs["l"] = stage[5], stage[6]

        def view(name, r, first_elem, count):
            skip = prev_skip if name in ("kp", "vp") else 0
            if coarse > 1:
                return srcs[name], (r % coarse, pl.ds(skip // coarse + r // coarse + fine * first_elem, count,
                                                       stride=fine), slice(None))
            if dil > 1:
                return srcs[name], (pl.ds(skip + r + dil * first_elem, count, stride=dil), slice(None))
            return srcs[name], (pl.ds(skip + first_elem, count), slice(None))

        def take(name, r, first_elem, count):
            ref, idx = view(name, r, first_elem, count)
            return ref[idx]

        def put(name, r, first_elem, value):
            ref, idx = view(name, r, first_elem, value.shape[0])
            ref[idx] = value

        for r in range(dil):
            for qb in range(rows_blk // (blk * dil)):
                mask = band_first if qb == 0 else band
                q2 = take("q", r, qb * blk, blk).astype(BF16)
                if qb == 0:
                    k2 = jnp.concatenate([take("kp", r, 0, blk), take("kc", r, 0, blk)], axis=0)
                    v2 = jnp.concatenate([take("vp", r, 0, blk), take("vc", r, 0, blk)], axis=0)
                else:
                    k2 = take("kc", r, (qb - 1) * blk, 2 * blk)
                    v2 = take("vc", r, (qb - 1) * blk, 2 * blk)
                k2, v2 = k2.astype(BF16), v2.astype(BF16)
                o_h, lse_h = [], []
                for hh in range(HEADS_PER_LANE_TILE):
                    s = lax.dot_general(q2 * qscale[hh], k2, (((1,), (1,)), ((), ())),
                                        preferred_element_type=F32)
                    s = jnp.where(mask, s, NEG_INF)
                    m = jnp.max(s, axis=-1, keepdims=True)
                    e = jnp.exp(s - m)
                    l = jnp.sum(e, axis=-1, keepdims=True)
                    o_h.append(jnp.dot(e.astype(BF16), v2, preferred_element_type=F32) / l)
                    lse_h.append(m + jnp.log(l))
                o_new = jnp.where(head0, o_h[0], o_h[1])
                lse_new = jnp.where(head0, lse_h[0], lse_h[1])
                if not first:
                    lse_old = take("l", r, qb * blk, blk)
                    mx = jnp.maximum(lse_old, lse_new)
                    w_old = jnp.exp(lse_old - mx)
                    w_new = jnp.exp(lse_new - mx)
                    tot = w_old + w_new
                    o_new = (take("o", r, qb * blk, blk) * w_old + o_new * w_new) / tot
                    lse_new = mx + jnp.log(tot)
                put("o", r, qb * blk, o_new)
                put("l", r, qb * blk, lse_new)

        if coarse > 1:
            for src, dst in ((stage[5], oacc_ref), (stage[6], lacc_ref)):
                for p in range(coarse):
                    dst[pl.ds(p, rows_blk // coarse, stride=coarse), :] = src[p]

    o_ref[...] = oacc_ref[...]


def _attention(q, k, v, patterns, super_block):
    B, n_slab, S, _ = q.shape
    assert S % super_block == 0 and all(super_block % (ATTN_BLOCK * d) == 0 for _, d in patterns)
    cur = pl.BlockSpec((None, None, super_block, LANES), lambda b, hp, n: (b, hp, n, 0))
    prev = pl.BlockSpec((None, None, super_block, LANES), lambda b, hp, n: (b, hp, jnp.maximum(n - 1, 0), 0))
    slab = pltpu.VMEM((super_block, LANES), F32)
    staged = pltpu.VMEM((COARSE_STRIDE, super_block // COARSE_STRIDE, LANES), F32)
    needs_stage = patterns[0][1] > COARSE_STRIDE
    return pl.pallas_call(
        functools.partial(_attn_kernel, patterns=tuple(patterns)),
        grid=(B, n_slab, S // super_block),
        in_specs=[cur, cur, cur, prev, prev],
        out_specs=cur,
        out_shape=jax.ShapeDtypeStruct(q.shape, F32),
        scratch_shapes=[slab, slab] + [staged] * (7 if needs_stage else 0),
        compiler_params=_params(("parallel", "parallel", "parallel")),
        name="attention",
    )(q, k, v, k, v)


def _mix_out_kernel(x_ref, attn_ref, glu_ref, halo_ref, cw_ref, cb_ref, lng_ref, lnb_ref,
                    ga_ref, gc_ref, wo_ref, gffn_ref, h_ref, xnt_ref, win_ref, rot_ref, *, tile):
    i = pl.program_id(1)
    halo = halo_ref[...]
    win_ref[0:CONV_HALO, :] = jnp.where(i > 0, halo, jnp.zeros_like(halo))
    win_ref[CONV_HALO:, :] = glu_ref[...]
    span = rot_ref.shape[1]
    for shift in range(1, SUB):
        rot_ref[shift - 1] = win_ref[pl.ds(shift, span), :]
    off = CONV_HALO - (CONV_WIDTH - 1)
    c = jnp.zeros((tile, D_CONV), F32)
    for j in range(CONV_WIDTH):
        shift, base = (off + j) % SUB, (off + j) // SUB * SUB
        rows = win_ref[base:base + tile, :] if shift == 0 else rot_ref[shift - 1, base:base + tile, :]
        c = c + rows * cw_ref[j:j + 1, :]
    c = c + cb_ref[...]
    mu = jnp.mean(c, axis=-1, keepdims=True)
    var = jnp.mean(jnp.square(c - mu), axis=-1, keepdims=True)
    c = (c - mu) * lax.rsqrt(var + EPS) * lng_ref[...] + lnb_ref[...]
    c = c * jax.nn.sigmoid(c)
    attn = jnp.concatenate([attn_ref[s] for s in range(D_ATTN // LANES)], axis=1)
    y = jnp.dot(_rms(attn, ga_ref[...]).astype(BF16), wo_ref[0:D_ATTN, :],
                preferred_element_type=F32)
    y = y + jnp.dot(_rms(c, gc_ref[...]).astype(BF16), wo_ref[D_ATTN:, :],
                    preferred_element_type=F32)
    h = x_ref[...] + y
    h_ref[...] = h
    xnt_ref[...] = _rms(h, gffn_ref[...]).T.astype(BF16)


def _mix_out(x, attn, glu, conv_w, conv_b, ln_g, ln_b, g_attn, g_conv, w_out, g_ffn, tile):
    B, S, _ = x.shape
    nt = S // tile
    hpt = tile // CONV_HALO
    row = lambda d: pl.BlockSpec((None, tile, d), lambda b, i: (b, i, 0))
    full = lambda shape: pl.BlockSpec(shape, lambda b, i: (0,) * len(shape))
    halo = pl.BlockSpec((None, CONV_HALO, D_CONV), lambda b, i: (b, jnp.maximum(i * hpt - 1, 0), 0))
    return pl.pallas_call(
        functools.partial(_mix_out_kernel, tile=tile),
        grid=(B, nt),
        in_specs=[row(D_MODEL),
                  pl.BlockSpec((None, D_ATTN // LANES, tile, LANES), lambda b, i: (b, 0, i, 0)),
                  row(D_CONV), halo,
                  full((CONV_WIDTH, D_CONV)), full((1, D_CONV)), full((1, D_CONV)), full((1, D_CONV)),
                  full((1, D_ATTN)), full((1, D_CONV)), full((D_MODEL, D_MODEL)), full((1, D_MODEL))],
        out_specs=[row(D_MODEL), pl.BlockSpec((D_MODEL, tile), lambda b, i: (0, b * nt + i))],
        out_shape=[jax.ShapeDtypeStruct((B, S, D_MODEL), F32),
                   jax.ShapeDtypeStruct((D_MODEL, B * S), BF16)],
        scratch_shapes=[pltpu.VMEM((CONV_HALO + tile, D_CONV), F32),
                        pltpu.VMEM((SUB - 1, CONV_HALO + tile - SUB, D_CONV), F32)],
        compiler_params=_params(("parallel", "parallel")),
        name="mix_out",
    )(x, attn, glu, glu, conv_w, conv_b, ln_g, ln_b, g_attn, g_conv, w_out, g_ffn)


def _sorting_network(n):
    pairs, p = [], 1
    while p < n:
        k = p
        while k >= 1:
            for j in range(k % p, n - k, 2 * k):
                for i in range(min(k, n - j - k)):
                    if (i + j) // (2 * p) == (i + j + k) // (2 * p):
                        pairs.append((i + j, i + j + k))
            k //= 2
        p *= 2
    return pairs


def _sort_desc(v, pairs):
    v = list(v)
    for i, j in pairs:
        v[i], v[j] = jnp.maximum(v[i], v[j]), jnp.minimum(v[i], v[j])
    return v


def _bitonic_sort_desc(v):
    v = list(v)
    d = len(v) // 2
    while d >= 1:
        for i in range(len(v)):
            if i & d == 0:
                v[i], v[i + d] = jnp.maximum(v[i], v[i + d]), jnp.minimum(v[i], v[i + d])
        d //= 2
    return v


def _merge_sublanes(v, shift):
    n = len(v)
    return _bitonic_sort_desc([jnp.maximum(v[k], pltpu.roll(v[n - 1 - k], shift, 0)) for k in range(n)])


def _top16_over_rows(tiles):
    v = _sort_desc(tiles, _sorting_network(len(tiles)))
    for shift in (4, 2, 1):
        v = _merge_sublanes(v, shift)
    return v


def _peer_select_kernel(xt_ref, wq_ref, keys_ref, w1_ref, c_ref, w2_ref, r2_ref, q_ref):
    n_lt = q_ref.shape[0]
    q = jnp.dot(wq_ref[...], xt_ref[...], preferred_element_type=F32).astype(BF16)
    for k in range(n_lt):
        q_ref[k] = q[:, k * LANES:(k + 1) * LANES]

    sub = lax.broadcasted_iota(jnp.int32, (SUB, LANES), 0)
    groups = N_KEYS // SUB

    def compact(v, lo):
        out = v[lo + SUB - 1]
        for s in range(SUB - 2, -1, -1):
            out = jnp.where(sub == s, v[lo + s], out)
        return out

    def head(h, lt):
        base = pl.multiple_of(h * 2 * N_KEYS, 2 * N_KEYS)
        s1 = jnp.dot(keys_ref[0], q_ref[lt, pl.ds(base, N_KEYS), :], preferred_element_type=F32)
        s2 = jnp.dot(keys_ref[1], q_ref[lt, pl.ds(base + N_KEYS, N_KEYS), :], preferred_element_type=F32)
        s1 = [s1[g * SUB:(g + 1) * SUB, :] for g in range(groups)]
        s2 = [s2[g * SUB:(g + 1) * SUB, :] for g in range(groups)]
        a = _top16_over_rows(s1)
        b = _top16_over_rows(s2)
        a_lo, a_hi, b_lo, b_hi = compact(a, 0), compact(a, SUB), compact(b, 0), compact(b, SUB)
        neg = jnp.full((SUB, LANES), NEG_INF, F32)
        c_j2 = jnp.where(sub < 5, a[2] + b_lo, neg)
        c_j3 = jnp.where(sub < 4, a[3] + b_lo, neg)
        c_j4 = jnp.where(sub < 3, a[4] + b_lo, neg)
        c_k0 = jnp.where(sub >= 5, a_lo + b[0], neg)
        c_k1 = jnp.where(sub >= 5, a_lo + b[1], neg)
        cands = [a[0] + b_lo, a[0] + b_hi, a[1] + b_lo,
                 jnp.where(sub < 5, c_j2, pltpu.roll(c_j4, 5, 0)),
                 jnp.where(sub < 4, c_j3, c_k0), c_k1, a_hi + b[0], neg]
        v = _sort_desc(cands, _sorting_network(len(cands)))
        v = _bitonic_sort_desc(v + [pltpu.roll(t, 4, 0) for t in reversed(v)])
        for shift in (2, 1):
            v = _merge_sublanes(v, shift)
        tau = v[PEER_TOPK - 1]
        top = a[0] + b[0]
        zsum = None
        for c in cands[:-1]:
            term = jnp.where(c >= tau, jnp.exp(c - top), 0.0)
            zsum = term if zsum is None else zsum + term
        for shift in (4, 2, 1):
            zsum = zsum + pltpu.roll(zsum, shift, 0)
        scale = 0.5 / zsum
        extra = []
        for j, kmax in ((0, 15), (1, 7), (2, 4)):
            e = jnp.zeros((SUB, LANES), F32)
            for k in range(4, kmax + 1):
                e = jnp.where(a[j] + b[k] >= tau, float(k - 3), e)
            extra.append(e)
        cnt, rank, w1, w2 = [], [], [], []
        for g in range(groups):
            c = jnp.zeros((SUB, LANES), F32)
            for k in range(4):
                c = jnp.where(s1[g] + b[k] >= tau, float(k + 1), c)
            c = c + jnp.where(s1[g] == a[0], extra[0],
                              jnp.where(s1[g] == a[1], extra[1], jnp.where(s1[g] == a[2], extra[2], 0.0)))
            r = jnp.zeros((SUB, LANES), F32)
            for k in range(PEER_TOPK):
                r = jnp.where(b[k] > s2[g], float(k + 1), r)
            cnt.append(c)
            rank.append(r)
            w1.append(jnp.exp(s1[g] - a[0]) * scale)
            w2.append(jnp.exp(s2[g] - b[0]))
        lanes = slice(lt * LANES, (lt + 1) * LANES)
        w1_ref[h, :, lanes] = jnp.concatenate(w1, axis=0)
        c_ref[h, :, lanes] = jnp.concatenate(cnt, axis=0)
        w2_ref[h, :, lanes] = jnp.concatenate(w2, axis=0).astype(BF16)
        r2_ref[h, :, lanes] = jnp.concatenate(rank, axis=0).astype(BF16)

    def head_group(i, carry, lt):
        for u in range(SELECT_HEADS_PER_TRIP):
            head(SELECT_HEADS_PER_TRIP * i + u, lt)
        return carry

    for lt in range(n_lt):
        lax.fori_loop(0, PEER_HEADS // SELECT_HEADS_PER_TRIP, functools.partial(head_group, lt=lt), 0)


def _peer_select(xnt, wq_t, keys, tile):
    n = xnt.shape[1]
    n_lt = tile // LANES
    tab = pl.BlockSpec((PEER_HEADS, N_KEYS, tile), lambda i: (0, 0, i))
    return pl.pallas_call(
        _peer_select_kernel,
        grid=(n // tile,),
        in_specs=[pl.BlockSpec((D_MODEL, tile), lambda i: (0, i)),
                  pl.BlockSpec(wq_t.shape, lambda i: (0, 0)),
                  pl.BlockSpec(keys.shape, lambda i: (0, 0, 0))],
        out_specs=[tab, tab, tab, tab],
        out_shape=[jax.ShapeDtypeStruct((PEER_HEADS, N_KEYS, n), F32)] * 2
        + [jax.ShapeDtypeStruct((PEER_HEADS, N_KEYS, n), BF16)] * 2,
        scratch_shapes=[pltpu.VMEM((n_lt, PEER_HEADS * 2 * N_KEYS, LANES), BF16)],
        compiler_params=_params(("parallel",)),
        name="peer_select",
    )(xnt, wq_t, keys)


def _peer_dense_kernel(xt_ref, u_ref, vt_ref, w1_ref, c_ref, w2_ref, r2_ref, h_ref, p_ref, gple_ref,
                       wg_ref, wp_ref, gfin_ref, o_ref, hid_ref, act_ref, acc_ref, *, last):
    j = pl.program_id(1)
    tile = xt_ref.shape[1]
    rows_per_chunk = u_ref.shape[0] // N_KEYS

    @pl.when(j == 0)
    def _():
        acc_ref[...] = jnp.zeros_like(acc_ref)

    hid_ref[...] = jnp.dot(u_ref[...], xt_ref[...], preferred_element_type=F32)
    for row in range(rows_per_chunk):
        for lp in range(tile // PAIR):
            lanes = slice(lp * PAIR, (lp + 1) * PAIR)

            def bcast(ref, h):
                parts = [jnp.broadcast_to(ref[h, row:row + 1, lp * PAIR + k * LANES:lp * PAIR + (k + 1) * LANES],
                                          (SUB, LANES)) for k in range(PAIR // LANES)]
                return jnp.concatenate(parts, axis=1).astype(BF16)

            gate = [jnp.zeros((SUB, PAIR), BF16) for _ in range(N_KEYS // SUB)]
            for h in range(PEER_HEADS):
                cnt = bcast(c_ref, h)
                w1 = bcast(w1_ref, h)
                for v in range(N_KEYS // SUB):
                    keys = slice(v * SUB, (v + 1) * SUB)
                    sel = jnp.where(r2_ref[h, keys, lanes] < cnt, w2_ref[h, keys, lanes],
                                    jnp.zeros((SUB, PAIR), BF16))
                    gate[v] = gate[v] + sel * w1
            for v in range(N_KEYS // SUB):
                rows = slice(row * N_KEYS + v * SUB, row * N_KEYS + (v + 1) * SUB)
                hid = hid_ref[rows, lanes]
                gelu2 = hid + hid * lax.erf(hid * math.sqrt(0.5))
                act_ref[rows, lanes] = gelu2.astype(BF16) * gate[v]
    acc_ref[...] += jnp.dot(vt_ref[...], act_ref[...], preferred_element_type=F32)

    @pl.when(j == pl.num_programs(1) - 1)
    def _():
        h = h_ref[...] + acc_ref[...].T
        gate = jax.nn.sigmoid(jnp.dot(_rms(h, gple_ref[...]).astype(BF16), wg_ref[...],
                                      preferred_element_type=F32))
        proj = jnp.dot(p_ref[...].astype(BF16), wp_ref[...], preferred_element_type=F32)
        h = h + gate * proj
        o_ref[...] = _rms(h, gfin_ref[...]) if last else h


def _peer_dense(xnt, u, vt, w1, cnt, w2, r2, h, p, g_ple, w_gate, w_proj, g_final, tile, chunk, last):
    n = xnt.shape[1]
    rows_per_chunk = chunk // N_KEYS
    small = pl.BlockSpec((PEER_HEADS, rows_per_chunk, tile), lambda i, j: (0, j, i))
    big = pl.BlockSpec((PEER_HEADS, N_KEYS, tile), lambda i, j: (0, 0, i))
    res = pl.BlockSpec((tile, D_MODEL), lambda i, j: (i, 0))
    once = lambda shape: pl.BlockSpec(shape, lambda i, j: (0,) * len(shape), pipeline_mode=pl.Buffered(1))
    return pl.pallas_call(
        functools.partial(_peer_dense_kernel, last=last),
        grid=(n // tile, N_EXPERTS // chunk),
        in_specs=[pl.BlockSpec((D_MODEL, tile), lambda i, j: (0, i)),
                  pl.BlockSpec((chunk, D_MODEL), lambda i, j: (j, 0)),
                  pl.BlockSpec((D_MODEL, chunk), lambda i, j: (0, j)),
                  small, small, big, big,
                  pl.BlockSpec((tile, D_MODEL), lambda i, j: (i, 0), pipeline_mode=pl.Buffered(1)),
                  pl.BlockSpec((tile, PLE_DIM), lambda i, j: (i, 0), pipeline_mode=pl.Buffered(1)),
                  once((1, D_MODEL)), once((D_MODEL, D_MODEL)), once((PLE_DIM, D_MODEL)), once((1, D_MODEL))],
        out_specs=res,
        out_shape=jax.ShapeDtypeStruct((n, D_MODEL), F32),
        scratch_shapes=[pltpu.VMEM((chunk, tile), F32), pltpu.VMEM((chunk, tile), BF16),
                        pltpu.VMEM((D_MODEL, tile), F32)],
        compiler_params=_params(("parallel", "arbitrary")),
        name="peer_dense",
    )(xnt, u, vt, w1, cnt, w2, r2, h, p, g_ple, w_gate, w_proj, g_final)


def _rope_tables(positions):
    half = ROPE_DIM // 2
    dim = jnp.arange(LANES) % HEAD_DIM
    inv_freq = jnp.where(dim < ROPE_DIM, ROPE_THETA ** (-2.0 * (dim % half).astype(F32) / ROPE_DIM), 0.0)
    ang = positions.astype(F32)[..., None] * inv_freq
    return jnp.cos(ang), jnp.sin(ang)


def kernel(x, p, positions, norm_mix, w_in, conv_w, conv_b, conv_ln_g, conv_ln_b, g_attn_out,
           g_conv_out, w_out, norm_ffn, peer_wq, sub_keys, expert_u, expert_v, norm_ple,
           w_ple_gate, w_ple_proj, final_norm):
    B, S, _ = x.shape
    depth = w_in.shape[0]
    n = B * S
    cos_t, sin_t = _rope_tables(positions)
    super_block = ATTN_BLOCK * max(d for _, d in DILATED_PATTERNS)
    vec = lambda t: t.reshape(1, -1)
    h = x
    for i in range(depth):
        q, k, v, glu = _mix_in(h, vec(norm_mix[i]), w_in[i].astype(BF16), cos_t, sin_t, tile=1024)
        attn = _attention(q, k, v, sorted(DILATED_PATTERNS, key=lambda wd: -wd[1]), super_block)
        h, xnt = _mix_out(h, attn, glu, conv_w[i], vec(conv_b[i]), vec(conv_ln_g[i]),
                          vec(conv_ln_b[i]), vec(g_attn_out[i]), vec(g_conv_out[i]),
                          w_out[i].astype(BF16), vec(norm_ffn[i]), tile=512)
        w1, cnt, w2, r2 = _peer_select(xnt, peer_wq[i].T.astype(BF16), sub_keys[i].astype(BF16),
                                       tile=512)
        h = _peer_dense(xnt, expert_u[i].astype(BF16), expert_v[i].T.astype(BF16), w1, cnt, w2, r2,
                        h.reshape(n, D_MODEL), p[i].reshape(n, PLE_DIM), vec(norm_ple[i]),
                        w_ple_gate[i].astype(BF16), w_ple_proj[i].astype(BF16), vec(final_norm),
                        tile=1024, chunk=1024, last=(i == depth - 1))
        h = h.reshape(B, S, D_MODEL)
    return h
```
